```python
import jax
import jax.numpy as jnp
from jax import lax
import numpy as np

D_MODEL = 1024
BATCH = 16
SEQ = 2048
DEPTH = 2

CHUNK = 64

D_MIX = D_MODEL
N_GROUPS = 4
GROUP_W = D_MIX // N_GROUPS
HEAD_DIM = 64

FOX_HEADS = GROUP_W // HEAD_DIM
FOX_DH = HEAD_DIM
Q_BLOCK = 128
FOX_BIAS_INIT = 2.0

GDN_HEADS = GROUP_W // HEAD_DIM
GDN_DK = HEAD_DIM
GDN_DV = HEAD_DIM
GDN_CONV = 4

HG_HEADS = GROUP_W // HEAD_DIM
HG_DK = HEAD_DIM
HG_DV = HEAD_DIM
HG_KW = HG_HEADS * HG_DK

GLA_HEADS = GROUP_W // HEAD_DIM
GLA_DV = HEAD_DIM
GLA_DK = HEAD_DIM // 2
GLA_KW = GLA_HEADS * GLA_DK
GLA_RANK = 16
GLA_NORM = 16.0

LA_BLOCK = 16

D_FF = ((8 * D_MODEL // 3 + 255) // 256) * 256
FFN_CONV = 3
EPS = 1e-6

IN_SIZES = (
    3 * GROUP_W, FOX_HEADS,
    3 * GROUP_W, GDN_HEADS, GDN_HEADS, GROUP_W,
    HG_KW, HG_KW, GROUP_W, GROUP_W,
    2 * GLA_KW, GROUP_W, GLA_RANK, GROUP_W,
)
IN_COLS = sum(IN_SIZES)

kernel_name = 'hybrid_fox_gdn_hgrn2_gla_convffn'


def _cut_points(sizes):
    cuts, acc = [], 0
    for s in sizes[:-1]:
        acc += s
        cuts.append(acc)
    return cuts


def rms_norm(x, g):
    xf = x.astype(jnp.float32)
    y = xf * lax.rsqrt(jnp.mean(xf * xf, axis=-1, keepdims=True) + EPS)
    return (y * g.astype(jnp.float32)).astype(x.dtype)


def l2norm(x):
    return x * lax.rsqrt(jnp.sum(x * x, axis=-1, keepdims=True) + EPS)


def split_heads(t, n):
    b, s, w = t.shape
    return t.reshape(b, s, n, w // n).transpose(0, 2, 1, 3)


def merge_heads(t):
    b, h, s, d = t.shape
    return t.transpose(0, 2, 1, 3).reshape(b, s, h * d)


def causal_dwconv(x, w):
    k_w = w.shape[0]
    s = x.shape[1]
    xp = jnp.pad(x, ((0, 0), (k_w - 1, 0), (0, 0)))
    y = xp[:, 0:s] * w[0]
    for j in range(1, k_w):
        y = y + xp[:, j:j + s] * w[j]
    return y


def fox_attention(q, k, v, log_f):
    b, h, s, d = q.shape
    c = jnp.cumsum(log_f, axis=-1)
    nb = s // Q_BLOCK
    qb = q.reshape(b, h, nb, Q_BLOCK, d).transpose(2, 0, 1, 3, 4)
    cb = c.reshape(b, h, nb, Q_BLOCK).transpose(2, 0, 1, 3)
    pos_k = jnp.arange(s)
    scale = d ** -0.5

    def one_block(args):
        qi, ci, bi = args
        pos_q = bi * Q_BLOCK + jnp.arange(Q_BLOCK)
        logits = jnp.einsum('bhqd,bhkd->bhqk', qi, k) * scale + ci[..., :, None] - c[:, :, None, :]
        logits = jnp.where(pos_k[None, :] <= pos_q[:, None], logits, -jnp.inf)
        probs = jax.nn.softmax(logits, axis=-1)
        return jnp.einsum('bhqk,bhkd->bhqd', probs, v)

    o = lax.map(one_block, (qb, cb, jnp.arange(nb)))
    return o.transpose(1, 2, 0, 3, 4).reshape(b, h, s, d)


def gated_delta_chunked(q, k, v, g, beta):
    b, h, s, dk = q.shape
    dv = v.shape[-1]
    c = CHUNK
    n = s // c
    q = (q * dk ** -0.5).reshape(b, h, n, c, dk)
    k = k.reshape(b, h, n, c, dk)
    v = v.reshape(b, h, n, c, dv)
    beta = beta.reshape(b, h, n, c)
    G = jnp.cumsum(g.reshape(b, h, n, c), axis=-1)
    causal = jnp.tril(jnp.ones((c, c), bool))
    strict = jnp.tril(jnp.ones((c, c), bool), -1)
    decay = jnp.exp(jnp.where(causal, G[..., :, None] - G[..., None, :], -jnp.inf))
    kb = k * beta[..., None]
    m = jnp.where(strict, jnp.einsum('bhnrd,bhnsd->bhnrs', kb, k) * decay, 0.0)
    t_mat = m + jnp.eye(c, dtype=m.dtype)
    w = lax.linalg.triangular_solve(t_mat, kb * jnp.exp(G)[..., None],
                                    left_side=True, lower=True, unit_diagonal=True)
    u = lax.linalg.triangular_solve(t_mat, v * beta[..., None],
                                    left_side=True, lower=True, unit_diagonal=True)
    a_qk = jnp.where(causal, jnp.einsum('bhnrd,bhnsd->bhnrs', q, k) * decay, 0.0)
    q_in = q * jnp.exp(G)[..., None]
    k_out = k * jnp.exp(G[..., -1:] - G)[..., None]
    a_chunk = jnp.exp(G[..., -1])

    def step(state, inp):
        qi, ki, wi, ui, ai_qk, ai = inp
        v_new = ui - jnp.einsum('bhcd,bhde->bhce', wi, state)
        o = jnp.einsum('bhcd,bhde->bhce', qi, state) + jnp.einsum('bhrs,bhse->bhre', ai_qk, v_new)
        state = state * ai[..., None, None] + jnp.einsum('bhcd,bhce->bhde', ki, v_new)
        return state, o

    xs = tuple(jnp.moveaxis(t, 2, 0) for t in (q_in, k_out, w, u, a_qk, a_chunk))
    _, o = lax.scan(step, jnp.zeros((b, h, dk, dv), q.dtype), xs)
    return jnp.moveaxis(o, 0, 2).reshape(b, h, s, dv)


def gla_chunked(q, k, v, log_a, scale):
    b, h, s, dk = q.shape
    dv = v.shape[-1]
    c = LA_BLOCK
    n = s // c
    q = (q * scale).reshape(b, h, n, c, dk)
    k = k.reshape(b, h, n, c, dk)
    v = v.reshape(b, h, n, c, dv)
    G = jnp.cumsum(log_a.reshape(b, h, n, c, dk), axis=3)
    causal = jnp.tril(jnp.ones((c, c), bool))
    rel = jnp.where(causal[:, :, None], G[:, :, :, :, None, :] - G[:, :, :, None, :, :], -jnp.inf)
    a_intra = jnp.einsum('bhnrd,bhnsd,bhnrsd->bhnrs', q, k, jnp.exp(rel))
    o_intra = jnp.einsum('bhnrs,bhnse->bhnre', a_intra, v)
    g_last = G[:, :, :, -1]
    q_in = q * jnp.exp(G)
    k_out = k * jnp.exp(g_last[:, :, :, None, :] - G)
    a_chunk = jnp.exp(g_last)

    def step(state, inp):
        qi, ki, vi, ai = inp
        o = jnp.einsum('bhcd,bhde->bhce', qi, state)
        state = state * ai[..., None] + jnp.einsum('bhcd,bhce->bhde', ki, vi)
        return state, o

    xs = tuple(jnp.moveaxis(t, 2, 0) for t in (q_in, k_out, v, a_chunk))
    _, o_inter = lax.scan(step, jnp.zeros((b, h, dk, dv), q.dtype), xs)
    return (o_intra + jnp.moveaxis(o_inter, 0, 2)).reshape(b, h, s, dv)


def hgrn_lower_bounds(lb_param):
    cs = jnp.cumsum(jax.nn.softmax(lb_param.astype(jnp.float32), axis=0), axis=0)
    return cs - cs[0:1]


def token_mix(h, w_in, fox_qn_g, fox_kn_g, fox_b_f, fox_on_g, gdn_conv_w, gdn_a_log,
              gdn_dt_bias, gdn_on_g, lb, hg_on_g, gla_w_gk, gla_b_gk, gla_on_g):
    p = (h @ w_in).astype(jnp.float32)
    (fox_qkv, fox_f, gdn_qkv, gdn_b, gdn_a, gdn_z, hg_q, hg_f, hg_i, hg_g,
     gla_qk, gla_v, gla_gk, gla_g) = jnp.split(p, _cut_points(IN_SIZES), axis=-1)

    q, k, v = jnp.split(fox_qkv, 3, axis=-1)
    q = rms_norm(split_heads(q, FOX_HEADS), fox_qn_g)
    k = rms_norm(split_heads(k, FOX_HEADS), fox_kn_g)
    log_f = jax.nn.log_sigmoid(fox_f + fox_b_f).transpose(0, 2, 1)
    o_a = rms_norm(fox_attention(q, k, split_heads(v, FOX_HEADS), log_f), fox_on_g)

    qkv = jax.nn.silu(causal_dwconv(gdn_qkv, gdn_conv_w))
    q, k, v = jnp.split(qkv, 3, axis=-1)
    beta = jax.nn.sigmoid(gdn_b).transpose(0, 2, 1)
    g = (-jnp.exp(gdn_a_log) * jax.nn.softplus(gdn_a + gdn_dt_bias)).transpose(0, 2, 1)
    o_b = gated_delta_chunked(l2norm(split_heads(q, GDN_HEADS)), l2norm(split_heads(k, GDN_HEADS)),
                              split_heads(v, GDN_HEADS), g, beta)
    o_b = rms_norm(o_b, gdn_on_g) * jax.nn.silu(split_heads(gdn_z, GDN_HEADS))

    lbh = lb.reshape(HG_HEADS, 1, HG_DK)
    f_logit = split_heads(hg_f, HG_HEADS)
    log_forget = jnp.logaddexp(jnp.log(lbh), jnp.log1p(-lbh) + jax.nn.log_sigmoid(f_logit))
    k_in = (1.0 - lbh) * jax.nn.sigmoid(-f_logit)
    o_c = gla_chunked(jax.nn.silu(split_heads(hg_q, HG_HEADS)), k_in, split_heads(hg_i, HG_HEADS),
                      log_forget, HG_DK ** -0.5)
    o_c = rms_norm(o_c, hg_on_g) * jax.nn.silu(split_heads(hg_g, HG_HEADS))

    q, k = jnp.split(gla_qk, 2, axis=-1)
    log_a = jax.nn.log_sigmoid(gla_gk @ gla_w_gk + gla_b_gk) / GLA_NORM
    o_d = gla_chunked(split_heads(q, GLA_HEADS), split_heads(k, GLA_HEADS), split_heads(gla_v, GLA_HEADS),
                      split_heads(log_a, GLA_HEADS), GLA_DK ** -0.5)
    o_d = rms_norm(o_d, gla_on_g) * jax.nn.silu(split_heads(gla_g, GLA_HEADS))

    return jnp.concatenate([merge_heads(o_a), merge_heads(o_b), merge_heads(o_c), merge_heads(o_d)], axis=-1)


def conv_glu_ffn(h, w_up, conv_w, conv_b, w_down):
    u = causal_dwconv(h @ w_up, conv_w) + conv_b
    gate, up = jnp.split(u, 2, axis=-1)
    return (jax.nn.silu(gate) * up) @ w_down


def setup_inputs(seed: int = 0) -> dict:
    key = jax.random.key(seed)
    ks = jax.random.split(key, 24)
    f32 = jnp.float32

    def nrm(k, shape, scale):
        return scale * jax.random.normal(k, shape, f32)

    def gain(k, n):
        return 1.0 + 0.1 * jax.random.normal(k, (DEPTH, n), f32)

    dt = jnp.exp(jax.random.uniform(ks[8], (DEPTH, GDN_HEADS), f32, jnp.log(1e-3), jnp.log(1e-1)))
    return {
        'x': nrm(ks[0], (BATCH, SEQ, D_MODEL), 1.0),
        'norm1_g': gain(ks[1], D_MODEL),
        'w_in': nrm(ks[2], (DEPTH, D_MODEL, IN_COLS), D_MODEL ** -0.5),
        'fox_qn_g': gain(ks[3], FOX_DH),
        'fox_kn_g': gain(ks[4], FOX_DH),
        'fox_b_f': FOX_BIAS_INIT + nrm(ks[5], (DEPTH, FOX_HEADS), 0.1),
        'fox_on_g': gain(ks[6], FOX_DH),
        'gdn_conv_w': nrm(ks[7], (DEPTH, GDN_CONV, 3 * GROUP_W), GDN_CONV ** -0.5),
        'gdn_a_log': jnp.log(jax.random.uniform(ks[9], (DEPTH, GDN_HEADS), f32, 1.0, 16.0)),
        'gdn_dt_bias': dt + jnp.log(-jnp.expm1(-dt)),
        'gdn_on_g': gain(ks[10], GDN_DV),
        'hg_lb': nrm(ks[11], (DEPTH, HG_KW), 0.1),
        'hg_on_g': gain(ks[12], HG_DV),
        'gla_w_gk': nrm(ks[13], (DEPTH, GLA_RANK, GLA_KW), GLA_RANK ** -0.5),
        'gla_b_gk': nrm(ks[14], (DEPTH, GLA_KW), 0.1),
        'gla_on_g': gain(ks[15], GLA_DV),
        'w_out': nrm(ks[16], (DEPTH, D_MIX, D_MODEL), D_MIX ** -0.5),
        'norm2_g': gain(ks[17], D_MODEL),
        'w_up': nrm(ks[18], (DEPTH, D_MODEL, 2 * D_FF), D_MODEL ** -0.5),
        'ffn_conv_w': nrm(ks[19], (DEPTH, FFN_CONV, 2 * D_FF), FFN_CONV ** -0.5),
        'ffn_conv_b': nrm(ks[20], (DEPTH, 2 * D_FF), 0.02),
        'w_down': nrm(ks[21], (DEPTH, D_FF, D_MODEL), D_FF ** -0.5),
    }


def reference(x, norm1_g, w_in, fox_qn_g, fox_kn_g, fox_b_f, fox_on_g, gdn_conv_w, gdn_a_log,
              gdn_dt_bias, gdn_on_g, hg_lb, hg_on_g, gla_w_gk, gla_b_gk, gla_on_g, w_out,
              norm2_g, w_up, ffn_conv_w, ffn_conv_b, w_down):
    lower_bounds = hgrn_lower_bounds(hg_lb)
    for l in range(DEPTH):
        mixed = token_mix(rms_norm(x, norm1_g[l]), w_in[l], fox_qn_g[l], fox_kn_g[l], fox_b_f[l],
                          fox_on_g[l], gdn_conv_w[l], gdn_a_log[l], gdn_dt_bias[l], gdn_on_g[l],
                          lower_bounds[l], hg_on_g[l], gla_w_gk[l], gla_b_gk[l], gla_on_g[l])
        x = x + mixed.astype(x.dtype) @ w_out[l]
        x = x + conv_glu_ffn(rms_norm(x, norm2_g[l]), w_up[l], ffn_conv_w[l], ffn_conv_b[l], w_down[l])
    return x
```

```python
import functools

import numpy as np
import jax
import jax.numpy as jnp
from jax import lax
from jax.experimental import pallas as pl
from jax.experimental.pallas import tpu as pltpu

F32 = jnp.float32
BF16 = jnp.bfloat16

D_MODEL = 1024
N_HEADS = 4
HEAD_DIM = 64
GROUP_W = N_HEADS * HEAD_DIM
GLA_DK = 32
GLA_KW = N_HEADS * GLA_DK
GLA_RANK = 16
GLA_NORM = 16.0
GDN_CONV = 4
D_FF = 2816
FFN_CONV = 3
EPS = 1e-6
CHUNK = 64

LANES = 128
GATE_W = LANES
GATE_FOX_F, GATE_GDN_B, GATE_GDN_A, GATE_GLA_GK = 0, 4, 8, 12

IN_SEGS = (3 * GROUP_W, 3 * GROUP_W, GROUP_W, 4 * GROUP_W, 3 * GROUP_W, GATE_W)
IN_COLS_PAD = sum(IN_SEGS)

ROW_TILE = 512
FF_CHUNK = 256
FOX_BLOCK = 256
NEG_BIG = -1e30

VMEM_LIMIT = 56 * 1024 * 1024


def _dot(a, b):
    return jnp.dot(a, b, preferred_element_type=F32)


def _dot_nt(a, b):
    return lax.dot_general(a, b, (((1,), (1,)), ((), ())), preferred_element_type=F32)


def _dot_tn(a, b):
    return lax.dot_general(a, b, (((0,), (0,)), ((), ())), preferred_element_type=F32)


def _split3(x):
    hi = x.astype(BF16)
    r1 = x - hi.astype(F32)
    mid = r1.astype(BF16)
    lo = (r1 - mid.astype(F32)).astype(BF16)
    return hi, mid, lo


def _dot01_l(m01, x):
    hi, mid, lo = _split3(x)
    return _dot(m01, hi) + _dot(m01, mid) + _dot(m01, lo)


def _dot01_r(x, m01):
    hi, mid, lo = _split3(x)
    return _dot(hi, m01) + _dot(mid, m01) + _dot(lo, m01)


def _seg_meansq(x, bd16, width):
    sq = x * x
    hi = sq.astype(BF16)
    lo = (sq - hi.astype(F32)).astype(BF16)
    return (_dot(hi, bd16) + _dot(lo, bd16)) * (1.0 / width)


def _log_sigmoid(x):
    return jnp.minimum(x, 0.0) - jnp.log1p(jnp.exp(-jnp.abs(x)))


def _softplus(x):
    return jnp.maximum(x, 0.0) + jnp.log1p(jnp.exp(-jnp.abs(x)))


def _silu(x):
    return x * jax.nn.sigmoid(x)


def _tile_rows(x, n):
    return jnp.concatenate([x] * n, axis=0)


def _full_spec(shape):
    nd = len(shape)
    return pl.BlockSpec(shape, lambda *_: (0,) * nd)


def _cparams(n_axes):
    return pltpu.CompilerParams(dimension_semantics=("arbitrary",) * n_axes, vmem_limit_bytes=VMEM_LIMIT)


def _block_mask(rows, row_blk, cols, col_blk):
    r = np.arange(rows)[:, None] // row_blk
    c = np.arange(cols)[None, :] // col_blk
    return (r == c).astype(np.float32)


def _tril_ones(n):
    return np.tril(np.ones((n, n), np.float32))


def _expand_mat(lane0):
    m = np.zeros((GATE_W, GROUP_W), np.float32)
    for h in range(N_HEADS):
        m[lane0 + h, h * HEAD_DIM:(h + 1) * HEAD_DIM] = 1.0
    return m


def _chunk_masks():
    r = np.arange(CHUNK)[:, None]
    s = np.arange(GROUP_W)[None, :] % CHUNK
    return ((r == s).astype(np.float32), (r > s).astype(np.float32), (r >= s).astype(np.float32))


LEVELS = (1, 2, 4, 8, 16, 32)


def _level_masks():
    r = np.arange(CHUNK)[:, None]
    s = np.arange(GROUP_W)[None, :] % CHUNK
    out = []
    for m in LEVELS:
        same = (r // (2 * m)) == (s // (2 * m))
        out.append((same & ((r % (2 * m)) >= m) & ((s % (2 * m)) < m)).astype(np.float32))
    return np.stack(out)


def _fox_routes():
    rq = np.zeros((N_HEADS, 3 * GATE_W, LANES), np.float32)
    rk = np.zeros((N_HEADS, 3 * GATE_W, LANES), np.float32)
    oq = np.zeros((N_HEADS, 1, LANES), np.float32)
    ok = np.zeros((N_HEADS, 1, LANES), np.float32)
    for h in range(N_HEADS):
        base = (1 - h % 2) * HEAD_DIM
        for j in range(3):
            rq[h, j * GATE_W + GATE_FOX_F + h, base + j] = 1.0
            rk[h, j * GATE_W + GATE_FOX_F + h, base + 3 + j] = -1.0
            oq[h, 0, base + 3 + j] = 1.0
            ok[h, 0, base + j] = 1.0
    return rq, rk, oq, ok


def _inproj_kernel(x_ref, g_ref, w_ref, *out_refs):
    x = x_ref[...]
    ms = jnp.mean(x * x, axis=-1, keepdims=True)
    h = (x * lax.rsqrt(ms + EPS) * g_ref[...]).astype(BF16)
    off = 0
    for ref, width in zip(out_refs, IN_SEGS):
        step = min(width, 256)
        for c in range(0, width, step):
            ref[:, c:c + step] = _dot(h, w_ref[:, off + c:off + c + step])
        off += width


def _inproj(x2d, g, w):
    t = x2d.shape[0]
    tm = min(ROW_TILE, t)
    return pl.pallas_call(
        _inproj_kernel,
        grid=(t // tm,),
        in_specs=[pl.BlockSpec((tm, D_MODEL), lambda i: (i, 0)),
                  _full_spec((1, D_MODEL)),
                  _full_spec((D_MODEL, IN_COLS_PAD))],
        out_specs=[pl.BlockSpec((tm, wd), lambda i: (i, 0)) for wd in IN_SEGS],
        out_shape=[jax.ShapeDtypeStruct((t, wd), F32) for wd in IN_SEGS],
        compiler_params=_cparams(1),
        name="inproj",
    )(x2d, g, w)


def _fox_prep_kernel(qkv_ref, gates_ref, bf_ref, qg_ref, kg_ref, bd_ref, ltri_ref, rq_ref, rk_ref, oq_ref, ok_ref,
                     qa_ref, ka_ref, vp_ref, *, rb):
    s_len = qkv_ref.shape[0]
    lane = lax.broadcasted_iota(jnp.int32, (rb, LANES), 1)

    def blk(i, carry):
        r0 = pl.multiple_of(i * rb, rb)
        logf = _log_sigmoid(gates_ref[pl.ds(r0, rb), :] + bf_ref[...])
        c = _dot01_l(ltri_ref[...], logf) + carry
        c3 = jnp.concatenate(_split3(c), axis=1)
        qkv = qkv_ref[pl.ds(r0, rb), :]
        q = qkv[:, 0:GROUP_W]
        k = qkv[:, GROUP_W:2 * GROUP_W]
        v = qkv[:, 2 * GROUP_W:3 * GROUP_W]
        bd = bd_ref[...]
        qn = q * lax.rsqrt(_seg_meansq(q, bd, HEAD_DIM) + EPS) * qg_ref[...] * (HEAD_DIM ** -0.5)
        kn = k * lax.rsqrt(_seg_meansq(k, bd, HEAD_DIM) + EPS) * kg_ref[...]
        for h in range(N_HEADS):
            p = h // 2
            own = (lane // HEAD_DIM) == (h % 2)
            qa = jnp.where(own, qn[:, p * LANES:(p + 1) * LANES], _dot(c3, rq_ref[h]) + oq_ref[h])
            ka = jnp.where(own, kn[:, p * LANES:(p + 1) * LANES], _dot(c3, rk_ref[h]) + ok_ref[h])
            qa_ref[0, h, pl.ds(r0, rb), :] = qa.astype(BF16)
            ka_ref[0, h, pl.ds(r0, rb), :] = ka.astype(BF16)
        for p in range(2):
            vp_ref[0, p, pl.ds(r0, rb), :] = v[:, p * LANES:(p + 1) * LANES].astype(BF16)
        return c[rb - 1:rb, :]

    lax.fori_loop(0, s_len // rb, blk, jnp.zeros((1, GATE_W), F32))


def _fox_prep(fox_qkv, gates, bf_row, qg_row, kg_row, consts, b, s):
    rb = min(256, s)
    rq, rk, oq, ok = consts["fox_routes"]
    ltri = jnp.asarray(_tril_ones(rb), BF16)
    kern = functools.partial(_fox_prep_kernel, rb=rb)
    return pl.pallas_call(
        kern,
        grid=(b,),
        in_specs=[pl.BlockSpec((s, 3 * GROUP_W), lambda i: (i, 0)),
                  pl.BlockSpec((s, GATE_W), lambda i: (i, 0)),
                  _full_spec((1, GATE_W)), _full_spec((1, GROUP_W)), _full_spec((1, GROUP_W)),
                  _full_spec((GROUP_W, GROUP_W)), _full_spec((rb, rb)),
                  _full_spec(rq.shape), _full_spec(rk.shape), _full_spec(oq.shape), _full_spec(ok.shape)],
        out_specs=[pl.BlockSpec((1, N_HEADS, s, LANES), lambda i: (i, 0, 0, 0)),
                   pl.BlockSpec((1, N_HEADS, s, LANES), lambda i: (i, 0, 0, 0)),
                   pl.BlockSpec((1, 2, s, LANES), lambda i: (i, 0, 0, 0))],
        out_shape=[jax.ShapeDtypeStruct((b, N_HEADS, s, LANES), BF16),
                   jax.ShapeDtypeStruct((b, N_HEADS, s, LANES), BF16),
                   jax.ShapeDtypeStruct((b, 2, s, LANES), BF16)],
        compiler_params=_cparams(1),
        name="fox_prep",
    )(fox_qkv, gates, bf_row, qg_row, kg_row, consts["bd256"], ltri, rq, rk, oq, ok)


def _fox_attn_kernel(q_ref, k_ref, v_ref, og_ref, o_ref, *, blk):
    iq = pl.program_id(2)
    lane = lax.broadcasted_iota(jnp.int32, (blk, LANES), 1)
    row = lax.broadcasted_iota(jnp.int32, (blk, blk), 0)
    col = lax.broadcasted_iota(jnp.int32, (blk, blk), 1)
    outs = []
    for hh in range(2):
        q = q_ref[0, hh]

        def step(j, carry, masked):
            m, l, acc = carry
            r0 = pl.multiple_of(j * blk, blk)
            s = _dot_nt(q, k_ref[0, hh, pl.ds(r0, blk), :])
            if masked:
                s = jnp.where(col <= row, s, NEG_BIG)
            m_new = jnp.maximum(m, jnp.max(s, axis=-1, keepdims=True))
            p = jnp.exp(s - m_new)
            alpha = jnp.exp(m - m_new)
            l = alpha * l + jnp.sum(p, axis=-1, keepdims=True)
            acc = alpha * acc + _dot(p.astype(BF16), v_ref[0, 0, pl.ds(r0, blk), :])
            return m_new, l, acc

        init = (jnp.full((blk, 1), NEG_BIG, F32), jnp.zeros((blk, 1), F32), jnp.zeros((blk, LANES), F32))
        carry = lax.fori_loop(0, iq, functools.partial(step, masked=False), init)
        _, l, acc = step(iq, carry, True)
        o = acc / l
        own = (lane // HEAD_DIM) == hh
        ms = jnp.sum(jnp.where(own, o * o, 0.0), axis=-1, keepdims=True) * (1.0 / HEAD_DIM)
        outs.append(o * lax.rsqrt(ms + EPS) * og_ref[...])
    o_ref[...] = jnp.where((lane // HEAD_DIM) == 0, outs[0], outs[1]).astype(BF16)


def _fox_attn(qa, ka, vp, og_row, b, s):
    blk = min(FOX_BLOCK, s)
    nq = s // blk
    kern = functools.partial(_fox_attn_kernel, blk=blk)
    return pl.pallas_call(
        kern,
        grid=(b, 2, nq),
        in_specs=[pl.BlockSpec((1, 2, blk, LANES), lambda i, p, j: (i, p, j, 0)),
                  pl.BlockSpec((1, 2, s, LANES), lambda i, p, j: (i, p, 0, 0)),
                  pl.BlockSpec((1, 1, s, LANES), lambda i, p, j: (i, p, 0, 0)),
                  _full_spec((1, LANES))],
        out_specs=pl.BlockSpec((blk, LANES), lambda i, p, j: (i * nq + j, p)),
        out_shape=jax.ShapeDtypeStruct((b * s, GROUP_W), BF16),
        compiler_params=_cparams(3),
        name="fox_attn",
    )(qa, ka, vp, og_row)


def _gdn_kernel(qkv_ref, z_ref, gates_ref, cw_ref, alog_ref, dt_ref, og_ref, bd_ref, eb_ref, eg_ref,
                eye_ref, strict_ref, causal_ref, ones_ref,
                o_ref, xpad, q_scr, k_scr, v_scr, beta_scr, g_scr, o_scr, s_scr, *, rb):
    s_len = qkv_ref.shape[0]
    nblk = s_len // rb
    bd16 = bd_ref[...]
    bdf = bd16.astype(F32)

    xpad[0:8, :] = jnp.zeros((8, 3 * GROUP_W), F32)
    xpad[8:, :] = qkv_ref[...]

    def conv_blk(i, _):
        r0 = pl.multiple_of(i * rb, rb)
        xx = xpad[pl.ds(r0, rb + 8), :]
        y = cw_ref[0:1, :] * xx[5:5 + rb]
        for j in range(1, GDN_CONV):
            y = y + cw_ref[j:j + 1, :] * xx[5 + j:5 + j + rb]
        y = _silu(y)
        q = y[:, 0:GROUP_W]
        k = y[:, GROUP_W:2 * GROUP_W]
        qn = q * lax.rsqrt(_seg_meansq(q, bd16, 1.0) + EPS)
        kn = k * lax.rsqrt(_seg_meansq(k, bd16, 1.0) + EPS)
        q_scr[pl.ds(r0, rb), :] = qn * (HEAD_DIM ** -0.5)
        k_scr[pl.ds(r0, rb), :] = kn
        v_scr[pl.ds(r0, rb), :] = y[:, 2 * GROUP_W:3 * GROUP_W]

        gt = gates_ref[pl.ds(r0, rb), :]
        beta_s = jax.nn.sigmoid(gt)
        g_s = -jnp.exp(alog_ref[...]) * _softplus(gt + dt_ref[...])
        rowc = lax.broadcasted_iota(jnp.int32, (rb, GATE_W), 0) % CHUNK
        sh = 1
        while sh < CHUNK:
            g_s = g_s + jnp.where(rowc >= sh, pltpu.roll(g_s, sh, 0), 0.0)
            sh *= 2
        beta_scr[pl.ds(r0, rb), :] = _dot01_r(beta_s, eb_ref[...])
        g_scr[pl.ds(r0, rb), :] = _dot01_r(g_s, eg_ref[...])
        return 0

    lax.fori_loop(0, nblk, conv_blk, 0)

    def bd(y16):
        return _tile_rows(y16, N_HEADS) * bd16

    s_scr[...] = jnp.zeros((GROUP_W, GROUP_W), F32)
    eye = eye_ref[...]

    def chunk(n, _):
        r0 = pl.multiple_of(n * CHUNK, CHUNK)
        q = q_scr[pl.ds(r0, CHUNK), :]
        k = k_scr[pl.ds(r0, CHUNK), :]
        v = v_scr[pl.ds(r0, CHUNK), :]
        beta = beta_scr[pl.ds(r0, CHUNK), :]
        g = g_scr[pl.ds(r0, CHUNK), :]
        g_last = g_scr[pl.ds(r0 + CHUNK - 1, 1), :]
        kb = k * beta
        kbd = bd(k.astype(BF16))
        aa = _dot_nt(jnp.concatenate([kb.astype(BF16), q.astype(BF16)], axis=0), kbd)
        g_row = _dot01_l(ones_ref[...], g * eye)
        decay = jnp.exp(jnp.minimum(g - g_row, 0.0))
        m = aa[0:CHUNK] * decay * strict_ref[...]
        a_qk = aa[CHUNK:2 * CHUNK] * decay * causal_ref[...]
        m16 = m.astype(BF16)
        pm = eye - m
        qm = _dot(m16, bd(m16))
        for it in range(5):
            q16 = qm.astype(BF16)
            if it < 4:
                r = _dot(jnp.concatenate([pm.astype(BF16), q16], axis=0), bd(q16))
                pm = pm + r[0:CHUNK]
                qm = r[CHUNK:2 * CHUNK]
            else:
                pm = pm + _dot(pm.astype(BF16), bd(q16))
        t16 = pm.astype(BF16)
        eg = jnp.exp(g)
        w = _dot(t16, bd((kb * eg).astype(BF16)))
        u = _dot(t16, bd((v * beta).astype(BF16)))
        st = s_scr[...]
        wq = _dot(jnp.concatenate([w.astype(BF16), (q * eg).astype(BF16)], axis=0), st.astype(BF16))
        v_new = u - wq[0:CHUNK]
        v16 = v_new.astype(BF16)
        o = wq[CHUNK:2 * CHUNK] + _dot(a_qk.astype(BF16), bd(v16))
        k_out = k * jnp.exp(g_last - g)
        s_scr[...] = st * jnp.exp(g_last) + _dot_tn(k_out.astype(BF16), v16) * bdf
        o_scr[pl.ds(r0, CHUNK), :] = o
        return 0

    lax.fori_loop(0, s_len // CHUNK, chunk, 0)

    def out_blk(i, _):
        r0 = pl.multiple_of(i * rb, rb)
        o = o_scr[pl.ds(r0, rb), :]
        on = o * lax.rsqrt(_seg_meansq(o, bd16, HEAD_DIM) + EPS) * og_ref[...]
        o_ref[pl.ds(r0, rb), :] = (on * _silu(z_ref[pl.ds(r0, rb), :])).astype(BF16)
        return 0

    lax.fori_loop(0, nblk, out_blk, 0)


def _gdn(gdn_qkv, gdn_z, gates, cw, alog_row, dt_row, og_row, consts, b, s):
    rb = min(256, s)
    eye, strict, causal = consts["chunk_masks"]
    kern = functools.partial(_gdn_kernel, rb=rb)
    row = lambda w: pl.BlockSpec((s, w), lambda i: (i, 0))
    return pl.pallas_call(
        kern,
        grid=(b,),
        in_specs=[row(3 * GROUP_W), row(GROUP_W), row(GATE_W),
                  _full_spec((8, 3 * GROUP_W)), _full_spec((1, GATE_W)), _full_spec((1, GATE_W)),
                  _full_spec((1, GROUP_W)), _full_spec((GROUP_W, GROUP_W)),
                  _full_spec((GATE_W, GROUP_W)), _full_spec((GATE_W, GROUP_W)),
                  _full_spec((CHUNK, GROUP_W)), _full_spec((CHUNK, GROUP_W)), _full_spec((CHUNK, GROUP_W)),
                  _full_spec((CHUNK, CHUNK))],
        out_specs=row(GROUP_W),
        out_shape=jax.ShapeDtypeStruct((b * s, GROUP_W), BF16),
        scratch_shapes=[pltpu.VMEM((s + 8, 3 * GROUP_W), F32)]
                       + [pltpu.VMEM((s, GROUP_W), F32) for _ in range(6)]
                       + [pltpu.VMEM((GROUP_W, GROUP_W), F32)],
        compiler_params=_cparams(1),
        name="gdn",
    )(gdn_qkv, gdn_z, gates, cw, alog_row, dt_row, og_row, consts["bd256"], consts["expand_b"], consts["expand_g"],
      eye, strict, causal, consts["ones64"])


def _la_kernel(*refs, variant, rb):
    if variant == "hgrn2":
        (x_ref, la_ref, l1_ref, oml_ref, og_ref, bdv_ref, bdk_ref, eye_ref, lvl_ref, ltri_ref,
         o_ref, q_scr, k_scr, a_scr, v_scr, gate_scr, gc_scr, o_scr, st_scr) = refs
        kw, dk = GROUP_W, HEAD_DIM
    else:
        (x_ref, gates_ref, wgk_ref, bgk_ref, og_ref, bdv_ref, bdk_ref, eye_ref, lvl_ref, ltri_ref,
         o_ref, q_scr, k_scr, a_scr, v_scr, gate_scr, gc_scr, o_scr, st_scr) = refs
        kw, dk = GLA_KW, GLA_DK
    s_len = x_ref.shape[0]
    nblk = s_len // rb
    bdv16 = bdv_ref[...]
    bdk16 = bdk_ref[...]
    bdkf = bdk16.astype(F32)

    def prep_blk(i, _):
        r0 = pl.multiple_of(i * rb, rb)
        x = x_ref[pl.ds(r0, rb), :]
        if variant == "hgrn2":
            f_logit = x[:, GROUP_W:2 * GROUP_W]
            q_scr[pl.ds(r0, rb), :] = _silu(x[:, 0:GROUP_W]) * (dk ** -0.5)
            a = la_ref[...]
            bterm = l1_ref[...] + _log_sigmoid(f_logit)
            amax = jnp.maximum(a, bterm)
            a_scr[pl.ds(r0, rb), :] = amax + jnp.log1p(jnp.exp(-jnp.abs(a - bterm)))
            k_scr[pl.ds(r0, rb), :] = oml_ref[...] * jax.nn.sigmoid(-f_logit)
            v_scr[pl.ds(r0, rb), :] = x[:, 2 * GROUP_W:3 * GROUP_W]
            gate_scr[pl.ds(r0, rb), :] = x[:, 3 * GROUP_W:4 * GROUP_W]
        else:
            q_scr[pl.ds(r0, rb), :] = x[:, 0:kw] * (dk ** -0.5)
            k_scr[pl.ds(r0, rb), :] = x[:, kw:2 * kw]
            v_scr[pl.ds(r0, rb), :] = x[:, 2 * kw:2 * kw + GROUP_W]
            gate_scr[pl.ds(r0, rb), :] = x[:, 2 * kw + GROUP_W:2 * kw + 2 * GROUP_W]
            lr = _dot(gates_ref[pl.ds(r0, rb), :].astype(BF16), wgk_ref[...]) + bgk_ref[...]
            a_scr[pl.ds(r0, rb), :] = _log_sigmoid(lr) * (1.0 / GLA_NORM)
        return 0

    lax.fori_loop(0, nblk, prep_blk, 0)

    st_scr[...] = jnp.zeros((GROUP_W, kw), F32)
    sub = lax.broadcasted_iota(jnp.int32, (8, kw), 0)

    def level_ref(m):
        def brow(r):
            return jnp.broadcast_to(gc_scr[r:r + 1, :], (8, kw))
        pieces = []
        for a in range(CHUNK // 8):
            if 2 * m >= 8:
                pieces.append(brow((8 * a) // (2 * m) * (2 * m) + m - 1))
            elif m == 2:
                pieces.append(jnp.where(sub < 4, brow(8 * a + 1), brow(8 * a + 5)))
            else:
                p = jnp.where(sub < 2, brow(8 * a), brow(8 * a + 2))
                p = jnp.where(sub < 4, p, jnp.where(sub < 6, brow(8 * a + 4), brow(8 * a + 6)))
                pieces.append(p)
        return jnp.concatenate(pieces, axis=0)

    def chunk(n, _):
        r0 = pl.multiple_of(n * CHUNK, CHUNK)
        q = q_scr[pl.ds(r0, CHUNK), :]
        k = k_scr[pl.ds(r0, CHUNK), :]
        v16 = v_scr[pl.ds(r0, CHUNK), :].astype(BF16)
        g = _dot01_l(ltri_ref[...], a_scr[pl.ds(r0, CHUNK), :])
        gc_scr[...] = g
        a_in = _dot_nt(q.astype(BF16), _tile_rows(k.astype(BF16), N_HEADS) * bdk16) * eye_ref[...]
        for li, m in enumerate(LEVELS):
            e = jnp.exp(-jnp.abs(g - level_ref(m)))
            qe = (q * e).astype(BF16)
            ke = (k * e).astype(BF16)
            a_in = a_in + _dot_nt(qe, _tile_rows(ke, N_HEADS) * bdk16) * lvl_ref[li]
        o = _dot(a_in.astype(BF16), _tile_rows(v16, N_HEADS) * bdv16)
        g_last = g[CHUNK - 1:CHUNK, :]
        st = st_scr[...]
        o = o + _dot_nt((q * jnp.exp(g)).astype(BF16), st.astype(BF16))
        k_out = (k * jnp.exp(g_last - g)).astype(BF16)
        st_scr[...] = st * jnp.exp(g_last) + _dot_tn(v16, k_out) * bdkf
        o_scr[pl.ds(r0, CHUNK), :] = o
        return 0

    lax.fori_loop(0, s_len // CHUNK, chunk, 0)

    def out_blk(i, _):
        r0 = pl.multiple_of(i * rb, rb)
        o = o_scr[pl.ds(r0, rb), :]
        on = o * lax.rsqrt(_seg_meansq(o, bdv16, HEAD_DIM) + EPS) * og_ref[...]
        o_ref[pl.ds(r0, rb), :] = (on * _silu(gate_scr[pl.ds(r0, rb), :])).astype(BF16)
        return 0

    lax.fori_loop(0, nblk, out_blk, 0)


def _la(variant, x, extra, og_row, consts, b, s):
    rb = min(256, s)
    kw = GROUP_W if variant == "hgrn2" else GLA_KW
    xw = 4 * GROUP_W if variant == "hgrn2" else 3 * GROUP_W
    kern = functools.partial(_la_kernel, variant=variant, rb=rb)
    row = lambda w: pl.BlockSpec((s, w), lambda i: (i, 0))
    if variant == "hgrn2":
        in_specs = [row(xw)] + [_full_spec((1, GROUP_W))] * 3
        bdk = consts["bd256"]
    else:
        in_specs = [row(xw), row(GATE_W), _full_spec((GATE_W, GLA_KW)), _full_spec((1, GLA_KW))]
        bdk = consts["bdk_gla"]
    in_specs += [_full_spec((1, GROUP_W)), _full_spec((GROUP_W, GROUP_W)), _full_spec((GROUP_W, kw)),
                 _full_spec((CHUNK, GROUP_W)), _full_spec((len(LEVELS), CHUNK, GROUP_W)), _full_spec((CHUNK, CHUNK))]
    return pl.pallas_call(
        kern,
        grid=(b,),
        in_specs=in_specs,
        out_specs=row(GROUP_W),
        out_shape=jax.ShapeDtypeStruct((b * s, GROUP_W), BF16),
        scratch_shapes=[pltpu.VMEM((s, kw), F32) for _ in range(3)]
                       + [pltpu.VMEM((s, GROUP_W), F32) for _ in range(2)]
                       + [pltpu.VMEM((CHUNK, kw), F32), pltpu.VMEM((s, GROUP_W), F32), pltpu.VMEM((GROUP_W, kw), F32)],
        compiler_params=_cparams(1),
        name="la_" + variant,
    )(x, *extra, og_row, consts["bd256"], bdk, consts["chunk_masks"][0], consts["level_masks"], consts["ltri64"])


FFN_TAIL = 16


def _ffn_kernel(x_ref, oa_ref, ob_ref, oc_ref, od_ref, wo_ref, g2_ref, wg_ref, wu_ref, cg_ref, cu_ref, wd_ref,
                out_ref, hext, ug_scr, uu_scr, acc_scr, *, tm):
    it = pl.program_id(1)
    x1 = x_ref[...]
    for kk, o_ref in enumerate((oa_ref, ob_ref, oc_ref, od_ref)):
        x1 = x1 + _dot(o_ref[...], wo_ref[kk])
    ms = jnp.mean(x1 * x1, axis=-1, keepdims=True)
    h2 = (x1 * lax.rsqrt(ms + EPS) * g2_ref[...]).astype(BF16)

    @pl.when(it == 0)
    def _():
        hext[0:FFN_TAIL, :] = jnp.zeros((FFN_TAIL, D_MODEL), BF16)

    hext[FFN_TAIL:, :] = h2
    acc_scr[...] = x1

    def conv(u_scr, c_ref, j):
        c = c_ref[j]
        y = c[0:1, :] * u_scr[pl.ds(FFN_TAIL - 2, tm), :]
        y = y + c[1:2, :] * u_scr[pl.ds(FFN_TAIL - 1, tm), :]
        y = y + c[2:3, :] * u_scr[pl.ds(FFN_TAIL, tm), :]
        return y + c[3:4, :]

    def ff(j, _):
        hx = hext[...]
        ug_scr[...] = _dot(hx, wg_ref[j])
        uu_scr[...] = _dot(hx, wu_ref[j])
        act = (_silu(conv(ug_scr, cg_ref, j)) * conv(uu_scr, cu_ref, j)).astype(BF16)
        acc_scr[...] += _dot(act, wd_ref[j])
        return 0

    lax.fori_loop(0, D_FF // FF_CHUNK, ff, 0)
    out_ref[...] = acc_scr[...]
    hext[0:FFN_TAIL, :] = hext[tm:tm + FFN_TAIL, :]


def _ffn(x2d, outs, wo, g2, wg, wu, cg, cu, wd, b, s):
    tm = min(ROW_TILE, s)
    nt = s // tm
    nff = D_FF // FF_CHUNK
    kern = functools.partial(_ffn_kernel, tm=tm)
    row = lambda w: pl.BlockSpec((tm, w), lambda i, j: (i * nt + j, 0))
    return pl.pallas_call(
        kern,
        grid=(b, nt),
        in_specs=[row(D_MODEL)] + [row(GROUP_W)] * 4
                 + [_full_spec((4, GROUP_W, D_MODEL)), _full_spec((1, D_MODEL)),
                    _full_spec((nff, D_MODEL, FF_CHUNK)), _full_spec((nff, D_MODEL, FF_CHUNK)),
                    _full_spec((nff, 8, FF_CHUNK)), _full_spec((nff, 8, FF_CHUNK)),
                    _full_spec((nff, FF_CHUNK, D_MODEL))],
        out_specs=row(D_MODEL),
        out_shape=jax.ShapeDtypeStruct(x2d.shape, F32),
        scratch_shapes=[pltpu.VMEM((tm + FFN_TAIL, D_MODEL), BF16),
                        pltpu.VMEM((tm + FFN_TAIL, FF_CHUNK), F32),
                        pltpu.VMEM((tm + FFN_TAIL, FF_CHUNK), F32),
                        pltpu.VMEM((tm, D_MODEL), F32)],
        compiler_params=_cparams(2),
        name="outproj_ffn",
    )(x2d, *outs, wo, g2, wg, wu, cg, cu, wd)


def _constants():
    eye, strict, causal = _chunk_masks()
    return {
        "bd256": jnp.asarray(_block_mask(GROUP_W, HEAD_DIM, GROUP_W, HEAD_DIM), BF16),
        "bdk_gla": jnp.asarray(_block_mask(GROUP_W, HEAD_DIM, GLA_KW, GLA_DK), BF16),
        "expand_b": jnp.asarray(_expand_mat(GATE_GDN_B), BF16),
        "expand_g": jnp.asarray(_expand_mat(GATE_GDN_A), BF16),
        "chunk_masks": (jnp.asarray(eye), jnp.asarray(strict), jnp.asarray(causal)),
        "level_masks": jnp.asarray(_level_masks()),
        "ones64": jnp.ones((CHUNK, CHUNK), BF16),
        "ltri64": jnp.asarray(_tril_ones(CHUNK), BF16),
        "fox_routes": tuple(jnp.asarray(a, BF16) if a.shape[1] > 1 else jnp.asarray(a) for a in _fox_routes()),
    }


def _permute_w_in(w_in):
    sizes = (3 * GROUP_W, N_HEADS, 3 * GROUP_W, N_HEADS, N_HEADS, GROUP_W,
             GROUP_W, GROUP_W, GROUP_W, GROUP_W, 2 * GLA_KW, GROUP_W, GLA_RANK, GROUP_W)
    cuts = np.concatenate([[0], np.cumsum(sizes)])
    seg = [w_in[:, cuts[i]:cuts[i + 1]] for i in range(len(sizes))]
    (fox_qkv, fox_f, gdn_qkv, gdn_b, gdn_a, gdn_z, hg_q, hg_f, hg_i, hg_g, gla_qk, gla_v, gla_gk, gla_g) = seg
    pad = jnp.zeros((w_in.shape[0], GATE_W - (3 * N_HEADS + GLA_RANK)), w_in.dtype)
    cols = [fox_qkv, gdn_qkv, gdn_z, hg_q, hg_f, hg_i, hg_g, gla_qk, gla_v, gla_g, fox_f, gdn_b, gdn_a, gla_gk, pad]
    return jnp.concatenate(cols, axis=1).astype(BF16)


def _lane_row(vals, lane0, width):
    return jnp.zeros((1, width), F32).at[0, lane0:lane0 + vals.shape[0]].set(vals.astype(F32))


def _tile_heads(g):
    return jnp.tile(g.astype(F32), N_HEADS)[None, :]


def kernel(x, norm1_g, w_in, fox_qn_g, fox_kn_g, fox_b_f, fox_on_g, gdn_conv_w, gdn_a_log, gdn_dt_bias, gdn_on_g,
           hg_lb, hg_on_g, gla_w_gk, gla_b_gk, gla_on_g, w_out, norm2_g, w_up, ffn_conv_w, ffn_conv_b, w_down):
    b, s, d = x.shape
    depth = w_in.shape[0]
    nff = D_FF // FF_CHUNK
    consts = _constants()

    cs = jnp.cumsum(jax.nn.softmax(hg_lb.astype(F32), axis=0), axis=0)
    lower = cs - cs[0:1]

    x2d = x.reshape(b * s, d)
    for l in range(depth):
        w_in_p = _permute_w_in(w_in[l])
        fox_qkv, gdn_qkv, gdn_z, hg, gla, gates = _inproj(x2d, norm1_g[l][None, :], w_in_p)

        qa, ka, vp = _fox_prep(fox_qkv, gates, _lane_row(fox_b_f[l], GATE_FOX_F, GATE_W),
                               _tile_heads(fox_qn_g[l]), _tile_heads(fox_kn_g[l]), consts, b, s)
        o_a = _fox_attn(qa, ka, vp, jnp.tile(fox_on_g[l].astype(F32), 2)[None, :], b, s)

        cw = jnp.zeros((8, 3 * GROUP_W), F32).at[0:GDN_CONV].set(gdn_conv_w[l])
        o_b = _gdn(gdn_qkv, gdn_z, gates, cw, _lane_row(gdn_a_log[l], GATE_GDN_A, GATE_W),
                   _lane_row(gdn_dt_bias[l], GATE_GDN_A, GATE_W), _tile_heads(gdn_on_g[l]), consts, b, s)

        lb = lower[l][None, :]
        o_c = _la("hgrn2", hg, (jnp.log(lb), jnp.log1p(-lb), 1.0 - lb), _tile_heads(hg_on_g[l]), consts, b, s)

        wgk = jnp.zeros((GATE_W, GLA_KW), F32).at[GATE_GLA_GK:GATE_GLA_GK + GLA_RANK].set(gla_w_gk[l]).astype(BF16)
        o_d = _la("gla", gla, (gates, wgk, gla_b_gk[l][None, :].astype(F32)), _tile_heads(gla_on_g[l]), consts, b, s)

        wo = w_out[l].reshape(4, GROUP_W, d).astype(BF16)
        wg = w_up[l][:, :D_FF].reshape(d, nff, FF_CHUNK).transpose(1, 0, 2).astype(BF16)
        wu = w_up[l][:, D_FF:].reshape(d, nff, FF_CHUNK).transpose(1, 0, 2).astype(BF16)
        taps = jnp.concatenate([ffn_conv_w[l], ffn_conv_b[l][None, :],
                                jnp.zeros((8 - FFN_CONV - 1, 2 * D_FF), F32)], axis=0)
        cg = taps[:, :D_FF].reshape(8, nff, FF_CHUNK).transpose(1, 0, 2)
        cu = taps[:, D_FF:].reshape(8, nff, FF_CHUNK).transpose(1, 0, 2)
        wd = w_down[l].reshape(nff, FF_CHUNK, d).astype(BF16)
        x2d = _ffn(x2d, (o_a, o_b, o_c, o_d), wo, norm2_g[l][None, :], wg, wu, cg, cu, wd, b, s)
    return x2d.reshape(b, s, d)
```

```python
import functools

import numpy as np
import jax
import jax.numpy as jnp
from jax import lax
from jax.experimental import pallas as pl
from jax.experimental.pallas import tpu as pltpu

F32 = jnp.float32
BF16 = jnp.bfloat16

D_MODEL = 1024
N_HEADS = 4
HEAD_DIM = 64
GROUP_W = N_HEADS * HEAD_DIM
GLA_DK = 32
GLA_KW = N_HEADS * GLA_DK
GLA_RANK = 16
GLA_NORM = 16.0
GDN_CONV = 4
D_FF = 2816
FFN_CONV = 3
EPS = 1e-6
CHUNK = 64

LANES = 128
GATE_W = LANES
GATE_FOX_F, GATE_GDN_B, GATE_GDN_A, GATE_GLA_GK = 0, 4, 8, 12

IN_SEGS = (3 * GROUP_W, 3 * GROUP_W, GROUP_W, 4 * GROUP_W, 3 * GROUP_W, GATE_W)
IN_COLS_PAD = sum(IN_SEGS)

ROW_TILE = 512
FF_CHUNK = 256
FOX_BLOCK = 512
NEG_BIG = -1e30

VMEM_LIMIT = 56 * 1024 * 1024


def _dot(a, b):
    return jnp.dot(a, b, preferred_element_type=F32)


def _dot_nt(a, b):
    return lax.dot_general(a, b, (((1,), (1,)), ((), ())), preferred_element_type=F32)


def _dot_tn(a, b):
    return lax.dot_general(a, b, (((0,), (0,)), ((), ())), preferred_element_type=F32)


def _split3(x):
    hi = x.astype(BF16)
    r1 = x - hi.astype(F32)
    mid = r1.astype(BF16)
    lo = (r1 - mid.astype(F32)).astype(BF16)
    return hi, mid, lo


def _dot01_l(m01, x):
    hi, mid, lo = _split3(x)
    return _dot(m01, hi) + _dot(m01, mid) + _dot(m01, lo)


def _dot01_r(x, m01):
    hi, mid, lo = _split3(x)
    return _dot(hi, m01) + _dot(mid, m01) + _dot(lo, m01)


def _seg_meansq(x, bd16, width):
    sq = x * x
    hi = sq.astype(BF16)
    lo = (sq - hi.astype(F32)).astype(BF16)
    return (_dot(hi, bd16) + _dot(lo, bd16)) * (1.0 / width)


def _log_sigmoid(x):
    return jnp.minimum(x, 0.0) - jnp.log1p(jnp.exp(-jnp.abs(x)))


def _softplus(x):
    return jnp.maximum(x, 0.0) + jnp.log1p(jnp.exp(-jnp.abs(x)))


def _silu(x):
    return x * jax.nn.sigmoid(x)


def _tile_rows(x, n):
    return jnp.concatenate([x] * n, axis=0)


def _full_spec(shape):
    nd = len(shape)
    return pl.BlockSpec(shape, lambda *_: (0,) * nd, pipeline_mode=pl.Buffered(1))


def _cparams(n_axes):
    return pltpu.CompilerParams(dimension_semantics=("arbitrary",) * n_axes, vmem_limit_bytes=VMEM_LIMIT)


def _block_mask(rows, row_blk, cols, col_blk):
    r = np.arange(rows)[:, None] // row_blk
    c = np.arange(cols)[None, :] // col_blk
    return (r == c).astype(np.float32)


def _tril_ones(n):
    return np.tril(np.ones((n, n), np.float32))


def _expand_mat(lane0):
    m = np.zeros((GATE_W, GROUP_W), np.float32)
    for h in range(N_HEADS):
        m[lane0 + h, h * HEAD_DIM:(h + 1) * HEAD_DIM] = 1.0
    return m


def _chunk_masks():
    r = np.arange(CHUNK)[:, None]
    s = np.arange(GROUP_W)[None, :] % CHUNK
    return ((r == s).astype(np.float32), (r > s).astype(np.float32), (r >= s).astype(np.float32))


LEVELS = (1, 2, 4, 8, 16, 32)


def _level_masks():
    r = np.arange(CHUNK)[:, None]
    s = np.arange(GROUP_W)[None, :] % CHUNK
    out = []
    for m in LEVELS:
        same = (r // (2 * m)) == (s // (2 * m))
        out.append((same & ((r % (2 * m)) >= m) & ((s % (2 * m)) < m)).astype(np.float32))
    return np.stack(out)


def _fox_routes():
    rq = np.zeros((N_HEADS, 3 * GATE_W, LANES), np.float32)
    rk = np.zeros((N_HEADS, 3 * GATE_W, LANES), np.float32)
    oq = np.zeros((N_HEADS, 1, LANES), np.float32)
    ok = np.zeros((N_HEADS, 1, LANES), np.float32)
    for h in range(N_HEADS):
        base = (1 - h % 2) * HEAD_DIM
        for j in range(3):
            rq[h, j * GATE_W + GATE_FOX_F + h, base + j] = 1.0
            rk[h, j * GATE_W + GATE_FOX_F + h, base + 3 + j] = -1.0
            oq[h, 0, base + 3 + j] = 1.0
            ok[h, 0, base + j] = 1.0
    return rq, rk, oq, ok


def _inproj_kernel(x_ref, g_ref, w_ref, *out_refs):
    x = x_ref[...]
    ms = jnp.mean(x * x, axis=-1, keepdims=True)
    h = (x * lax.rsqrt(ms + EPS) * g_ref[...]).astype(BF16)
    off = 0
    for ref, width in zip(out_refs, IN_SEGS):
        step = min(width, 256)
        for c in range(0, width, step):
            ref[:, c:c + step] = _dot(h, w_ref[:, off + c:off + c + step])
        off += width


def _inproj(x2d, g, w):
    t = x2d.shape[0]
    tm = min(ROW_TILE, t)
    return pl.pallas_call(
        _inproj_kernel,
        grid=(t // tm,),
        in_specs=[pl.BlockSpec((tm, D_MODEL), lambda i: (i, 0)),
                  _full_spec((1, D_MODEL)),
                  _full_spec((D_MODEL, IN_COLS_PAD))],
        out_specs=[pl.BlockSpec((tm, wd), lambda i: (i, 0)) for wd in IN_SEGS],
        out_shape=[jax.ShapeDtypeStruct((t, wd), F32) for wd in IN_SEGS],
        compiler_params=_cparams(1),
        name="inproj",
    )(x2d, g, w)


def _fox_prep_kernel(qkv_ref, gates_ref, bf_ref, qg_ref, kg_ref, bd_ref, ltri_ref, rq_ref, rk_ref, oq_ref, ok_ref,
                     qa_ref, ka_ref, va_ref, *, rb):
    s_len = qkv_ref.shape[0]
    lane = lax.broadcasted_iota(jnp.int32, (rb, LANES), 1)

    def blk(i, carry):
        r0 = pl.multiple_of(i * rb, rb)
        logf = _log_sigmoid(gates_ref[pl.ds(r0, rb), :] + bf_ref[...])
        c = _dot01_l(ltri_ref[...], logf) + carry
        c3 = jnp.concatenate(_split3(c), axis=1)
        qkv = qkv_ref[pl.ds(r0, rb), :]
        q = qkv[:, 0:GROUP_W]
        k = qkv[:, GROUP_W:2 * GROUP_W]
        v = qkv[:, 2 * GROUP_W:3 * GROUP_W]
        bd = bd_ref[...]
        qn = q * lax.rsqrt(_seg_meansq(q, bd, HEAD_DIM) + EPS) * qg_ref[...] * (HEAD_DIM ** -0.5)
        kn = k * lax.rsqrt(_seg_meansq(k, bd, HEAD_DIM) + EPS) * kg_ref[...]
        for h in range(N_HEADS):
            p = h // 2
            own = (lane // HEAD_DIM) == (h % 2)
            qa = jnp.where(own, qn[:, p * LANES:(p + 1) * LANES], _dot(c3, rq_ref[h]) + oq_ref[h])
            ka = jnp.where(own, kn[:, p * LANES:(p + 1) * LANES], _dot(c3, rk_ref[h]) + ok_ref[h])
            va = jnp.where(own, v[:, p * LANES:(p + 1) * LANES], jnp.where(lane == (1 - h % 2) * HEAD_DIM, 1.0, 0.0))
            qa_ref[0, h, pl.ds(r0, rb), :] = qa.astype(BF16)
            ka_ref[0, h, pl.ds(r0, rb), :] = ka.astype(BF16)
            va_ref[0, h, pl.ds(r0, rb), :] = va.astype(BF16)
        return c[rb - 1:rb, :]

    lax.fori_loop(0, s_len // rb, blk, jnp.zeros((1, GATE_W), F32))


def _fox_prep(fox_qkv, gates, bf_row, qg_row, kg_row, consts, b, s):
    rb = min(256, s)
    rq, rk, oq, ok = consts["fox_routes"]
    ltri = jnp.asarray(_tril_ones(rb), BF16)
    kern = functools.partial(_fox_prep_kernel, rb=rb)
    return pl.pallas_call(
        kern,
        grid=(b,),
        in_specs=[pl.BlockSpec((s, 3 * GROUP_W), lambda i: (i, 0)),
                  pl.BlockSpec((s, GATE_W), lambda i: (i, 0)),
                  _full_spec((1, GATE_W)), _full_spec((1, GROUP_W)), _full_spec((1, GROUP_W)),
                  _full_spec((GROUP_W, GROUP_W)), _full_spec((rb, rb)),
                  _full_spec(rq.shape), _full_spec(rk.shape), _full_spec(oq.shape), _full_spec(ok.shape)],
        out_specs=[pl.BlockSpec((1, N_HEADS, s, LANES), lambda i: (i, 0, 0, 0)),
                   pl.BlockSpec((1, N_HEADS, s, LANES), lambda i: (i, 0, 0, 0)),
                   pl.BlockSpec((1, N_HEADS, s, LANES), lambda i: (i, 0, 0, 0))],
        out_shape=[jax.ShapeDtypeStruct((b, N_HEADS, s, LANES), BF16)] * 3,
        compiler_params=_cparams(1),
        name="fox_prep",
    )(fox_qkv, gates, bf_row, qg_row, kg_row, consts["bd256"], ltri, rq, rk, oq, ok)


def _fox_attn_kernel(q_ref, k_ref, v_ref, og_ref, o_ref, s_scr, m_scr, acc_scr, *, blk):
    iq = pl.program_id(2)
    lane = lax.broadcasted_iota(jnp.int32, (blk, LANES), 1)
    row = lax.broadcasted_iota(jnp.int32, (blk, blk), 0)
    col = lax.broadcasted_iota(jnp.int32, (blk, blk), 1)

    def fold(s):
        m = s[:, 0:LANES]
        for c in range(1, blk // LANES):
            m = jnp.maximum(m, s[:, c * LANES:(c + 1) * LANES])
        return m

    def logits(hh, j, masked):
        r0 = pl.multiple_of(j * blk, blk)
        s = _dot_nt(q_ref[0, hh], k_ref[0, hh, pl.ds(r0, blk), :])
        if masked:
            s = jnp.where(col <= row, s, NEG_BIG)
        s_scr[hh, j] = s
        m_scr[hh] = jnp.maximum(m_scr[hh], fold(s))

    m_scr[...] = jnp.full(m_scr.shape, NEG_BIG, F32)

    def p1(j, _):
        for hh in range(2):
            logits(hh, j, False)
        return 0

    lax.fori_loop(0, iq, p1, 0)
    for hh in range(2):
        logits(hh, iq, True)

    row_max = [jnp.max(m_scr[hh], axis=-1, keepdims=True) for hh in range(2)]
    acc_scr[...] = jnp.zeros(acc_scr.shape, F32)

    def p2(j, _):
        r0 = pl.multiple_of(j * blk, blk)
        for hh in range(2):
            p = jnp.exp(s_scr[hh, j] - row_max[hh])
            acc_scr[hh] += _dot(p.astype(BF16), v_ref[0, hh, pl.ds(r0, blk), :])
        return 0

    lax.fori_loop(0, iq + 1, p2, 0)

    outs = []
    for hh in range(2):
        acc = acc_scr[hh]
        l = jnp.sum(jnp.where(lane == (1 - hh) * HEAD_DIM, acc, 0.0), axis=-1, keepdims=True)
        o = acc / l
        own = (lane // HEAD_DIM) == hh
        ms = jnp.sum(jnp.where(own, o * o, 0.0), axis=-1, keepdims=True) * (1.0 / HEAD_DIM)
        outs.append(o * lax.rsqrt(ms + EPS) * og_ref[...])
    o_ref[...] = jnp.where((lane // HEAD_DIM) == 0, outs[0], outs[1]).astype(BF16)


def _fox_attn(qa, ka, va, og_row, b, s):
    blk = min(FOX_BLOCK, s)
    nq = s // blk
    kern = functools.partial(_fox_attn_kernel, blk=blk)
    return pl.pallas_call(
        kern,
        grid=(b, 2, nq),
        in_specs=[pl.BlockSpec((1, 2, blk, LANES), lambda i, p, j: (i, p, j, 0)),
                  pl.BlockSpec((1, 2, s, LANES), lambda i, p, j: (i, p, 0, 0)),
                  pl.BlockSpec((1, 2, s, LANES), lambda i, p, j: (i, p, 0, 0)),
                  _full_spec((1, LANES))],
        out_specs=pl.BlockSpec((blk, LANES), lambda i, p, j: (i * nq + j, p)),
        out_shape=jax.ShapeDtypeStruct((b * s, GROUP_W), BF16),
        scratch_shapes=[pltpu.VMEM((2, nq, blk, blk), F32),
                        pltpu.VMEM((2, blk, LANES), F32),
                        pltpu.VMEM((2, blk, LANES), F32)],
        compiler_params=_cparams(3),
        name="fox_attn",
    )(qa, ka, va, og_row)


def _gdn_kernel(qkv_ref, z_ref, gates_ref, cw_ref, alog_ref, dt_ref, og_ref, bd_ref, eb_ref, eg_ref,
                eye_ref, strict_ref, causal_ref, ones_ref,
                o_ref, xpad, q_scr, k_scr, v_scr, beta_scr, g_scr, o_scr, s_scr, *, rb):
    s_len = qkv_ref.shape[0]
    nblk = s_len // rb
    bd16 = bd_ref[...]
    bdf = bd16.astype(F32)

    xpad[0:8, :] = jnp.zeros((8, 3 * GROUP_W), F32)
    xpad[8:, :] = qkv_ref[...]

    def conv_blk(i, _):
        r0 = pl.multiple_of(i * rb, rb)
        xx = xpad[pl.ds(r0, rb + 8), :]
        y = cw_ref[0:1, :] * xx[5:5 + rb]
        for j in range(1, GDN_CONV):
            y = y + cw_ref[j:j + 1, :] * xx[5 + j:5 + j + rb]
        y = _silu(y)
        q = y[:, 0:GROUP_W]
        k = y[:, GROUP_W:2 * GROUP_W]
        qn = q * lax.rsqrt(_seg_meansq(q, bd16, 1.0) + EPS)
        kn = k * lax.rsqrt(_seg_meansq(k, bd16, 1.0) + EPS)
        q_scr[pl.ds(r0, rb), :] = qn * (HEAD_DIM ** -0.5)
        k_scr[pl.ds(r0, rb), :] = kn
        v_scr[pl.ds(r0, rb), :] = y[:, 2 * GROUP_W:3 * GROUP_W]

        gt = gates_ref[pl.ds(r0, rb), :]
        beta_s = jax.nn.sigmoid(gt)
        g_s = -jnp.exp(alog_ref[...]) * _softplus(gt + dt_ref[...])
        rowc = lax.broadcasted_iota(jnp.int32, (rb, GATE_W), 0) % CHUNK
        sh = 1
        while sh < CHUNK:
            g_s = g_s + jnp.where(rowc >= sh, pltpu.roll(g_s, sh, 0), 0.0)
            sh *= 2
        beta_scr[pl.ds(r0, rb), :] = _dot01_r(beta_s, eb_ref[...])
        g_scr[pl.ds(r0, rb), :] = _dot01_r(g_s, eg_ref[...])
        return 0

    lax.fori_loop(0, nblk, conv_blk, 0)

    def bd(y16):
        return _tile_rows(y16, N_HEADS) * bd16

    s_scr[...] = jnp.zeros((GROUP_W, GROUP_W), F32)
    eye = eye_ref[...]

    def chunk(n, _):
        r0 = pl.multiple_of(n * CHUNK, CHUNK)
        q = q_scr[pl.ds(r0, CHUNK), :]
        k = k_scr[pl.ds(r0, CHUNK), :]
        v = v_scr[pl.ds(r0, CHUNK), :]
        beta = beta_scr[pl.ds(r0, CHUNK), :]
        g = g_scr[pl.ds(r0, CHUNK), :]
        g_last = g_scr[pl.ds(r0 + CHUNK - 1, 1), :]
        kb = k * beta
        kbd = bd(k.astype(BF16))
        aa = _dot_nt(jnp.concatenate([kb.astype(BF16), q.astype(BF16)], axis=0), kbd)
        g_row = _dot01_l(ones_ref[...], g * eye)
        decay = jnp.exp(jnp.minimum(g - g_row, 0.0))
        m = aa[0:CHUNK] * decay * strict_ref[...]
        a_qk = aa[CHUNK:2 * CHUNK] * decay * causal_ref[...]
        m16 = m.astype(BF16)
        pm = eye - m
        qm = _dot(m16, bd(m16))
        for it in range(5):
            q16 = qm.astype(BF16)
            if it < 4:
                r = _dot(jnp.concatenate([pm.astype(BF16), q16], axis=0), bd(q16))
                pm = pm + r[0:CHUNK]
                qm = r[CHUNK:2 * CHUNK]
            else:
                pm = pm + _dot(pm.astype(BF16), bd(q16))
        t16 = pm.astype(BF16)
        eg = jnp.exp(g)
        w = _dot(t16, bd((kb * eg).astype(BF16)))
        u = _dot(t16, bd((v * beta).astype(BF16)))
        st = s_scr[...]
        wq = _dot(jnp.concatenate([w.astype(BF16), (q * eg).astype(BF16)], axis=0), st.astype(BF16))
        v_new = u - wq[0:CHUNK]
        v16 = v_new.astype(BF16)
        o = wq[CHUNK:2 * CHUNK] + _dot(a_qk.astype(BF16), bd(v16))
        k_out = k * jnp.exp(g_last - g)
        s_scr[...] = st * jnp.exp(g_last) + _dot_tn(k_out.astype(BF16), v16) * bdf
        o_scr[pl.ds(r0, CHUNK), :] = o
        return 0

    lax.fori_loop(0, s_len // CHUNK, chunk, 0)

    def out_blk(i, _):
        r0 = pl.multiple_of(i * rb, rb)
        o = o_scr[pl.ds(r0, rb), :]
        on = o * lax.rsqrt(_seg_meansq(o, bd16, HEAD_DIM) + EPS) * og_ref[...]
        o_ref[pl.ds(r0, rb), :] = (on * _silu(z_ref[pl.ds(r0, rb), :])).astype(BF16)
        return 0

    lax.fori_loop(0, nblk, out_blk, 0)


def _gdn(gdn_qkv, gdn_z, gates, cw, alog_row, dt_row, og_row, consts, b, s):
    rb = min(256, s)
    eye, strict, causal = consts["chunk_masks"]
    kern = functools.partial(_gdn_kernel, rb=rb)
    row = lambda w: pl.BlockSpec((s, w), lambda i: (i, 0))
    return pl.pallas_call(
        kern,
        grid=(b,),
        in_specs=[row(3 * GROUP_W), row(GROUP_W), row(GATE_W),
                  _full_spec((8, 3 * GROUP_W)), _full_spec((1, GATE_W)), _full_spec((1, GATE_W)),
                  _full_spec((1, GROUP_W)), _full_spec((GROUP_W, GROUP_W)),
                  _full_spec((GATE_W, GROUP_W)), _full_spec((GATE_W, GROUP_W)),
                  _full_spec((CHUNK, GROUP_W)), _full_spec((CHUNK, GROUP_W)), _full_spec((CHUNK, GROUP_W)),
                  _full_spec((CHUNK, CHUNK))],
        out_specs=row(GROUP_W),
        out_shape=jax.ShapeDtypeStruct((b * s, GROUP_W), BF16),
        scratch_shapes=[pltpu.VMEM((s + 8, 3 * GROUP_W), F32)]
                       + [pltpu.VMEM((s, GROUP_W), F32) for _ in range(6)]
                       + [pltpu.VMEM((GROUP_W, GROUP_W), F32)],
        compiler_params=_cparams(1),
        name="gdn",
    )(gdn_qkv, gdn_z, gates, cw, alog_row, dt_row, og_row, consts["bd256"], consts["expand_b"], consts["expand_g"],
      eye, strict, causal, consts["ones64"])


def _la_kernel(*refs, variant, rb):
    if variant == "hgrn2":
        (x_ref, la_ref, l1_ref, oml_ref, og_ref, bdv_ref, bdk_ref, eye_ref, lvl_ref, ltri_ref,
         o_ref, q_scr, k_scr, a_scr, v_scr, gate_scr, gc_scr, o_scr, st_scr) = refs
        kw, dk = GROUP_W, HEAD_DIM
    else:
        (x_ref, gates_ref, wgk_ref, bgk_ref, og_ref, bdv_ref, bdk_ref, eye_ref, lvl_ref, ltri_ref,
         o_ref, q_scr, k_scr, a_scr, v_scr, gate_scr, gc_scr, o_scr, st_scr) = refs
        kw, dk = GLA_KW, GLA_DK
    s_len = x_ref.shape[0]
    nblk = s_len // rb
    bdv16 = bdv_ref[...]
    bdk16 = bdk_ref[...]
    bdkf = bdk16.astype(F32)

    def prep_blk(i, _):
        r0 = pl.multiple_of(i * rb, rb)
        x = x_ref[pl.ds(r0, rb), :]
        if variant == "hgrn2":
            f_logit = x[:, GROUP_W:2 * GROUP_W]
            q_scr[pl.ds(r0, rb), :] = _silu(x[:, 0:GROUP_W]) * (dk ** -0.5)
            a = la_ref[...]
            bterm = l1_ref[...] + _log_sigmoid(f_logit)
            amax = jnp.maximum(a, bterm)
            a_scr[pl.ds(r0, rb), :] = amax + jnp.log1p(jnp.exp(-jnp.abs(a - bterm)))
            k_scr[pl.ds(r0, rb), :] = oml_ref[...] * jax.nn.sigmoid(-f_logit)
            v_scr[pl.ds(r0, rb), :] = x[:, 2 * GROUP_W:3 * GROUP_W]
            gate_scr[pl.ds(r0, rb), :] = x[:, 3 * GROUP_W:4 * GROUP_W]
        else:
            q_scr[pl.ds(r0, rb), :] = x[:, 0:kw] * (dk ** -0.5)
            k_scr[pl.ds(r0, rb), :] = x[:, kw:2 * kw]
            v_scr[pl.ds(r0, rb), :] = x[:, 2 * kw:2 * kw + GROUP_W]
            gate_scr[pl.ds(r0, rb), :] = x[:, 2 * kw + GROUP_W:2 * kw + 2 * GROUP_W]
            lr = _dot(gates_ref[pl.ds(r0, rb), :].astype(BF16), wgk_ref[...]) + bgk_ref[...]
            a_scr[pl.ds(r0, rb), :] = _log_sigmoid(lr) * (1.0 / GLA_NORM)
        return 0

    lax.fori_loop(0, nblk, prep_blk, 0)

    st_scr[...] = jnp.zeros((GROUP_W, kw), F32)
    sub = lax.broadcasted_iota(jnp.int32, (8, kw), 0)

    def level_ref(m):
        def brow(r):
            return jnp.broadcast_to(gc_scr[r:r + 1, :], (8, kw))
        pieces = []
        for a in range(CHUNK // 8):
            if 2 * m >= 8:
                pieces.append(brow((8 * a) // (2 * m) * (2 * m) + m - 1))
            elif m == 2:
                pieces.append(jnp.where(sub < 4, brow(8 * a + 1), brow(8 * a + 5)))
            else:
                p = jnp.where(sub < 2, brow(8 * a), brow(8 * a + 2))
                p = jnp.where(sub < 4, p, jnp.where(sub < 6, brow(8 * a + 4), brow(8 * a + 6)))
                pieces.append(p)
        return jnp.concatenate(pieces, axis=0)

    def chunk(n, _):
        r0 = pl.multiple_of(n * CHUNK, CHUNK)
        q = q_scr[pl.ds(r0, CHUNK), :]
        k = k_scr[pl.ds(r0, CHUNK), :]
        v16 = v_scr[pl.ds(r0, CHUNK), :].astype(BF16)
        g = _dot01_l(ltri_ref[...], a_scr[pl.ds(r0, CHUNK), :])
        gc_scr[...] = g
        a_in = _dot_nt(q.astype(BF16), _tile_rows(k.astype(BF16), N_HEADS) * bdk16) * eye_ref[...]
        for li, m in enumerate(LEVELS):
            e = jnp.exp(-jnp.abs(g - level_ref(m)))
            qe = (q * e).astype(BF16)
            ke = (k * e).astype(BF16)
            a_in = a_in + _dot_nt(qe, _tile_rows(ke, N_HEADS) * bdk16) * lvl_ref[li]
        o = _dot(a_in.astype(BF16), _tile_rows(v16, N_HEADS) * bdv16)
        g_last = g[CHUNK - 1:CHUNK, :]
        st = st_scr[...]
        o = o + _dot_nt((q * jnp.exp(g)).astype(BF16), st.astype(BF16))
        k_out = (k * jnp.exp(g_last - g)).astype(BF16)
        st_scr[...] = st * jnp.exp(g_last) + _dot_tn(v16, k_out) * bdkf
        o_scr[pl.ds(r0, CHUNK), :] = o
        return 0

    lax.fori_loop(0, s_len // CHUNK, chunk, 0)

    def out_blk(i, _):
        r0 = pl.multiple_of(i * rb, rb)
        o = o_scr[pl.ds(r0, rb), :]
        on = o * lax.rsqrt(_seg_meansq(o, bdv16, HEAD_DIM) + EPS) * og_ref[...]
        o_ref[pl.ds(r0, rb), :] = (on * _silu(gate_scr[pl.ds(r0, rb), :])).astype(BF16)
        return 0

    lax.fori_loop(0, nblk, out_blk, 0)


def _la(variant, x, extra, og_row, consts, b, s):
    rb = min(256, s)
    kw = GROUP_W if variant == "hgrn2" else GLA_KW
    xw = 4 * GROUP_W if variant == "hgrn2" else 3 * GROUP_W
    kern = functools.partial(_la_kernel, variant=variant, rb=rb)
    row = lambda w: pl.BlockSpec((s, w), lambda i: (i, 0))
    if variant == "hgrn2":
        in_specs = [row(xw)] + [_full_spec((1, GROUP_W))] * 3
        bdk = consts["bd256"]
    else:
        in_specs = [row(xw), row(GATE_W), _full_spec((GATE_W, GLA_KW)), _full_spec((1, GLA_KW))]
        bdk = consts["bdk_gla"]
    in_specs += [_full_spec((1, GROUP_W)), _full_spec((GROUP_W, GROUP_W)), _full_spec((GROUP_W, kw)),
                 _full_spec((CHUNK, GROUP_W)), _full_spec((len(LEVELS), CHUNK, GROUP_W)), _full_spec((CHUNK, CHUNK))]
    return pl.pallas_call(
        kern,
        grid=(b,),
        in_specs=in_specs,
        out_specs=row(GROUP_W),
        out_shape=jax.ShapeDtypeStruct((b * s, GROUP_W), BF16),
        scratch_shapes=[pltpu.VMEM((s, kw), F32) for _ in range(3)]
                       + [pltpu.VMEM((s, GROUP_W), F32) for _ in range(2)]
                       + [pltpu.VMEM((CHUNK, kw), F32), pltpu.VMEM((s, GROUP_W), F32), pltpu.VMEM((GROUP_W, kw), F32)],
        compiler_params=_cparams(1),
        name="la_" + variant,
    )(x, *extra, og_row, consts["bd256"], bdk, consts["chunk_masks"][0], consts["level_masks"], consts["ltri64"])


FFN_TAIL = 16


def _ffn_kernel(x_ref, oa_ref, ob_ref, oc_ref, od_ref, wo_ref, g2_ref, wup_ref, ctap_ref, wd_ref,
                out_ref, hext, u_scr, act_scr, *, tm):
    it = pl.program_id(1)
    x1 = x_ref[...]
    for kk, o_ref in enumerate((oa_ref, ob_ref, oc_ref, od_ref)):
        x1 = x1 + _dot(o_ref[...], wo_ref[kk])
    out_ref[...] = x1
    ms = jnp.mean(x1 * x1, axis=-1, keepdims=True)
    h2 = (x1 * lax.rsqrt(ms + EPS) * g2_ref[...]).astype(BF16)

    @pl.when(it == 0)
    def _():
        hext[0:FFN_TAIL, :] = jnp.zeros((FFN_TAIL, D_MODEL), BF16)

    hext[FFN_TAIL:, :] = h2

    def conv(col0, slot):
        u_scr[slot] = _dot(hext[...], wup_ref[:, col0:col0 + FF_CHUNK])
        c = ctap_ref[:, col0:col0 + FF_CHUNK]
        y = c[0:1, :] * u_scr[slot, FFN_TAIL - 2:FFN_TAIL - 2 + tm, :]
        y = y + c[1:2, :] * u_scr[slot, FFN_TAIL - 1:FFN_TAIL - 1 + tm, :]
        y = y + c[2:3, :] * u_scr[slot, FFN_TAIL:FFN_TAIL + tm, :]
        return y + c[3:4, :]

    for j in range(D_FF // FF_CHUNK):
        slot = 2 * (j % 2)
        gate = conv(j * FF_CHUNK, slot)
        up = conv(D_FF + j * FF_CHUNK, slot + 1)
        act_scr[:, j * FF_CHUNK:(j + 1) * FF_CHUNK] = (_silu(gate) * up).astype(BF16)
    out_ref[...] += _dot(act_scr[...], wd_ref[...])
    hext[0:FFN_TAIL, :] = hext[tm:tm + FFN_TAIL, :]


def _ffn(x2d, outs, wo, g2, wup, ctap, wd, b, s):
    tm = min(ROW_TILE, s)
    nt = s // tm
    kern = functools.partial(_ffn_kernel, tm=tm)
    row = lambda w: pl.BlockSpec((tm, w), lambda i, j: (i * nt + j, 0))
    return pl.pallas_call(
        kern,
        grid=(b, nt),
        in_specs=[row(D_MODEL)] + [row(GROUP_W)] * 4
                 + [_full_spec((4, GROUP_W, D_MODEL)), _full_spec((1, D_MODEL)),
                    _full_spec((D_MODEL, 2 * D_FF)), _full_spec((8, 2 * D_FF)), _full_spec((D_FF, D_MODEL))],
        out_specs=row(D_MODEL),
        out_shape=jax.ShapeDtypeStruct(x2d.shape, F32),
        scratch_shapes=[pltpu.VMEM((tm + FFN_TAIL, D_MODEL), BF16),
                        pltpu.VMEM((4, tm + FFN_TAIL, FF_CHUNK), F32),
                        pltpu.VMEM((tm, D_FF), BF16)],
        compiler_params=_cparams(2),
        name="outproj_ffn",
    )(x2d, *outs, wo, g2, wup, ctap, wd)


def _constants():
    eye, strict, causal = _chunk_masks()
    return {
        "bd256": jnp.asarray(_block_mask(GROUP_W, HEAD_DIM, GROUP_W, HEAD_DIM), BF16),
        "bdk_gla": jnp.asarray(_block_mask(GROUP_W, HEAD_DIM, GLA_KW, GLA_DK), BF16),
        "expand_b": jnp.asarray(_expand_mat(GATE_GDN_B), BF16),
        "expand_g": jnp.asarray(_expand_mat(GATE_GDN_A), BF16),
        "chunk_masks": (jnp.asarray(eye), jnp.asarray(strict), jnp.asarray(causal)),
        "level_masks": jnp.asarray(_level_masks()),
        "ones64": jnp.ones((CHUNK, CHUNK), BF16),
        "ltri64": jnp.asarray(_tril_ones(CHUNK), BF16),
        "fox_routes": tuple(jnp.asarray(a, BF16) if a.shape[1] > 1 else jnp.asarray(a) for a in _fox_routes()),
    }


def _permute_w_in(w_in):
    sizes = (3 * GROUP_W, N_HEADS, 3 * GROUP_W, N_HEADS, N_HEADS, GROUP_W,
             GROUP_W, GROUP_W, GROUP_W, GROUP_W, 2 * GLA_KW, GROUP_W, GLA_RANK, GROUP_W)
    cuts = np.concatenate([[0], np.cumsum(sizes)])
    seg = [w_in[:, cuts[i]:cuts[i + 1]] for i in range(len(sizes))]
    (fox_qkv, fox_f, gdn_qkv, gdn_b, gdn_a, gdn_z, hg_q, hg_f, hg_i, hg_g, gla_qk, gla_v, gla_gk, gla_g) = seg
    pad = jnp.zeros((w_in.shape[0], GATE_W - (3 * N_HEADS + GLA_RANK)), w_in.dtype)
    cols = [fox_qkv, gdn_qkv, gdn_z, hg_q, hg_f, hg_i, hg_g, gla_qk, gla_v, gla_g, fox_f, gdn_b, gdn_a, gla_gk, pad]
    return jnp.concatenate(cols, axis=1).astype(BF16)


def _lane_row(vals, lane0, width):
    return jnp.zeros((1, width), F32).at[0, lane0:lane0 + vals.shape[0]].set(vals.astype(F32))


def _tile_heads(g):
    return jnp.tile(g.astype(F32), N_HEADS)[None, :]


def kernel(x, norm1_g, w_in, fox_qn_g, fox_kn_g, fox_b_f, fox_on_g, gdn_conv_w, gdn_a_log, gdn_dt_bias, gdn_on_g,
           hg_lb, hg_on_g, gla_w_gk, gla_b_gk, gla_on_g, w_out, norm2_g, w_up, ffn_conv_w, ffn_conv_b, w_down):
    b, s, d = x.shape
    depth = w_in.shape[0]
    consts = _constants()

    cs = jnp.cumsum(jax.nn.softmax(hg_lb.astype(F32), axis=0), axis=0)
    lower = cs - cs[0:1]

    x2d = x.reshape(b * s, d)
    for l in range(depth):
        w_in_p = _permute_w_in(w_in[l])
        fox_qkv, gdn_qkv, gdn_z, hg, gla, gates = _inproj(x2d, norm1_g[l][None, :], w_in_p)

        qa, ka, va = _fox_prep(fox_qkv, gates, _lane_row(fox_b_f[l], GATE_FOX_F, GATE_W),
                               _tile_heads(fox_qn_g[l]), _tile_heads(fox_kn_g[l]), consts, b, s)
        o_a = _fox_attn(qa, ka, va, jnp.tile(fox_on_g[l].astype(F32), 2)[None, :], b, s)

        cw = jnp.zeros((8, 3 * GROUP_W), F32).at[0:GDN_CONV].set(gdn_conv_w[l])
        o_b = _gdn(gdn_qkv, gdn_z, gates, cw, _lane_row(gdn_a_log[l], GATE_GDN_A, GATE_W),
                   _lane_row(gdn_dt_bias[l], GATE_GDN_A, GATE_W), _tile_heads(gdn_on_g[l]), consts, b, s)

        lb = lower[l][None, :]
        o_c = _la("hgrn2", hg, (jnp.log(lb), jnp.log1p(-lb), 1.0 - lb), _tile_heads(hg_on_g[l]), consts, b, s)

        wgk = jnp.zeros((GATE_W, GLA_KW), F32).at[GATE_GLA_GK:GATE_GLA_GK + GLA_RANK].set(gla_w_gk[l]).astype(BF16)
        o_d = _la("gla", gla, (gates, wgk, gla_b_gk[l][None, :].astype(F32)), _tile_heads(gla_on_g[l]), consts, b, s)

        wo = w_out[l].reshape(4, GROUP_W, d).astype(BF16)
        taps = jnp.concatenate([ffn_conv_w[l], ffn_conv_b[l][None, :],
                                jnp.zeros((8 - FFN_CONV - 1, 2 * D_FF), F32)], axis=0)
        x2d = _ffn(x2d, (o_a, o_b, o_c, o_d), wo, norm2_g[l][None, :], w_up[l].astype(BF16), taps,
                   w_down[l].astype(BF16), b, s)
    return x2d.reshape(b, s, d)
```

```python
import functools

import numpy as np
import jax
import jax.numpy as jnp
from jax import lax
from jax.experimental import pallas as pl
from jax.experimental.pallas import tpu as pltpu

F32 = jnp.float32
BF16 = jnp.bfloat16

D_MODEL = 1024
N_HEADS = 4
HEAD_DIM = 64
GROUP_W = N_HEADS * HEAD_DIM
GLA_DK = 32
GLA_KW = N_HEADS * GLA_DK
GLA_RANK = 16
GLA_NORM = 16.0
GDN_CONV = 4
D_FF = 2816
FFN_CONV = 3
EPS = 1e-6
CHUNK = 64
WY_GROUP = 4

LANES = 128
GATE_W = LANES
GATE_FOX_F, GATE_GDN_B, GATE_GDN_A, GATE_GLA_GK = 0, 4, 8, 12

IN_SEGS = (3 * GROUP_W, 3 * GROUP_W, GROUP_W, 4 * GROUP_W, 3 * GROUP_W, GATE_W)
IN_COLS_PAD = sum(IN_SEGS)

ROW_TILE = 512
FF_CHUNK = 256
FOX_BLOCK = 512
NEG_BIG = -1e30

VMEM_LIMIT = 56 * 1024 * 1024


def _dot(a, b):
    return jnp.dot(a, b, preferred_element_type=F32)


def _dot_nt(a, b):
    return lax.dot_general(a, b, (((1,), (1,)), ((), ())), preferred_element_type=F32)


def _dot_tn(a, b):
    return lax.dot_general(a, b, (((0,), (0,)), ((), ())), preferred_element_type=F32)


def _split3(x):
    hi = x.astype(BF16)
    r1 = x - hi.astype(F32)
    mid = r1.astype(BF16)
    lo = (r1 - mid.astype(F32)).astype(BF16)
    return hi, mid, lo


def _dot01_l(m01, x):
    hi, mid, lo = _split3(x)
    return _dot(m01, hi) + _dot(m01, mid) + _dot(m01, lo)


def _dot01_r(x, m01):
    hi, mid, lo = _split3(x)
    return _dot(hi, m01) + _dot(mid, m01) + _dot(lo, m01)


def _seg_meansq(x, bd16, width):
    sq = x * x
    hi = sq.astype(BF16)
    lo = (sq - hi.astype(F32)).astype(BF16)
    return (_dot(hi, bd16) + _dot(lo, bd16)) * (1.0 / width)


def _log_sigmoid(x):
    return jnp.minimum(x, 0.0) - jnp.log1p(jnp.exp(-jnp.abs(x)))


def _softplus(x):
    return jnp.maximum(x, 0.0) + jnp.log1p(jnp.exp(-jnp.abs(x)))


def _silu(x):
    return x * jax.nn.sigmoid(x)


def _tile_rows(x, n):
    return jnp.concatenate([x] * n, axis=0)


def _full_spec(shape):
    nd = len(shape)
    return pl.BlockSpec(shape, lambda *_: (0,) * nd, pipeline_mode=pl.Buffered(1))


def _cparams(n_axes):
    return pltpu.CompilerParams(dimension_semantics=("arbitrary",) * n_axes, vmem_limit_bytes=VMEM_LIMIT)


def _block_mask(rows, row_blk, cols, col_blk):
    r = np.arange(rows)[:, None] // row_blk
    c = np.arange(cols)[None, :] // col_blk
    return (r == c).astype(np.float32)


def _tril_ones(n):
    return np.tril(np.ones((n, n), np.float32))


def _expand_mat(lane0):
    m = np.zeros((GATE_W, GROUP_W), np.float32)
    for h in range(N_HEADS):
        m[lane0 + h, h * HEAD_DIM:(h + 1) * HEAD_DIM] = 1.0
    return m


def _chunk_masks():
    r = np.arange(CHUNK)[:, None]
    s = np.arange(GROUP_W)[None, :] % CHUNK
    return ((r == s).astype(np.float32), (r > s).astype(np.float32), (r >= s).astype(np.float32))


LEVELS = (1, 2, 4, 8, 16, 32)


def _level_masks():
    r = np.arange(CHUNK)[:, None]
    s = np.arange(GROUP_W)[None, :] % CHUNK
    out = []
    for m in LEVELS:
        same = (r // (2 * m)) == (s // (2 * m))
        out.append((same & ((r % (2 * m)) >= m) & ((s % (2 * m)) < m)).astype(np.float32))
    return np.stack(out)


def _fox_routes():
    rq = np.zeros((N_HEADS, 3 * GATE_W, LANES), np.float32)
    rk = np.zeros((N_HEADS, 3 * GATE_W, LANES), np.float32)
    oq = np.zeros((N_HEADS, 1, LANES), np.float32)
    ok = np.zeros((N_HEADS, 1, LANES), np.float32)
    for h in range(N_HEADS):
        base = (1 - h % 2) * HEAD_DIM
        for j in range(3):
            rq[h, j * GATE_W + GATE_FOX_F + h, base + j] = 1.0
            rk[h, j * GATE_W + GATE_FOX_F + h, base + 3 + j] = -1.0
            oq[h, 0, base + 3 + j] = 1.0
            ok[h, 0, base + j] = 1.0
    return rq, rk, oq, ok


def _inproj_kernel(x_ref, g_ref, w_ref, *out_refs):
    x = x_ref[...]
    ms = jnp.mean(x * x, axis=-1, keepdims=True)
    h = (x * lax.rsqrt(ms + EPS) * g_ref[...]).astype(BF16)
    off = 0
    for ref, width in zip(out_refs, IN_SEGS):
        step = min(width, 256)
        for c in range(0, width, step):
            ref[:, c:c + step] = _dot(h, w_ref[:, off + c:off + c + step])
        off += width


def _inproj(x2d, g, w_all, layer):
    t = x2d.shape[0]
    tm = min(ROW_TILE, t)
    return pl.pallas_call(
        _inproj_kernel,
        grid=(t // tm,),
        in_specs=[pl.BlockSpec((tm, D_MODEL), lambda i: (i, 0)),
                  _full_spec((1, D_MODEL)),
                  pl.BlockSpec((None, D_MODEL, IN_COLS_PAD), lambda i: (layer, 0, 0), pipeline_mode=pl.Buffered(1))],
        out_specs=[pl.BlockSpec((tm, wd), lambda i: (i, 0)) for wd in IN_SEGS],
        out_shape=[jax.ShapeDtypeStruct((t, wd), F32) for wd in IN_SEGS],
        compiler_params=_cparams(1),
        name="inproj",
    )(x2d, g, w_all)


def _fox_prep_kernel(qkv_ref, gates_ref, bf_ref, qg_ref, kg_ref, bd_ref, ltri_ref, rq_ref, rk_ref, oq_ref, ok_ref,
                     qa_ref, ka_ref, va_ref, *, rb):
    s_len = qkv_ref.shape[0]
    lane = lax.broadcasted_iota(jnp.int32, (rb, LANES), 1)

    def blk(i, carry):
        r0 = pl.multiple_of(i * rb, rb)
        logf = _log_sigmoid(gates_ref[pl.ds(r0, rb), :] + bf_ref[...])
        c = _dot01_l(ltri_ref[...], logf) + carry
        c3 = jnp.concatenate(_split3(c), axis=1)
        qkv = qkv_ref[pl.ds(r0, rb), :]
        q = qkv[:, 0:GROUP_W]
        k = qkv[:, GROUP_W:2 * GROUP_W]
        v = qkv[:, 2 * GROUP_W:3 * GROUP_W]
        bd = bd_ref[...]
        qn = q * lax.rsqrt(_seg_meansq(q, bd, HEAD_DIM) + EPS) * qg_ref[...] * (HEAD_DIM ** -0.5)
        kn = k * lax.rsqrt(_seg_meansq(k, bd, HEAD_DIM) + EPS) * kg_ref[...]
        for h in range(N_HEADS):
            p = h // 2
            own = (lane // HEAD_DIM) == (h % 2)
            qa = jnp.where(own, qn[:, p * LANES:(p + 1) * LANES], _dot(c3, rq_ref[h]) + oq_ref[h])
            ka = jnp.where(own, kn[:, p * LANES:(p + 1) * LANES], _dot(c3, rk_ref[h]) + ok_ref[h])
            va = jnp.where(own, v[:, p * LANES:(p + 1) * LANES], jnp.where(lane == (1 - h % 2) * HEAD_DIM, 1.0, 0.0))
            qa_ref[0, h, pl.ds(r0, rb), :] = qa.astype(BF16)
            ka_ref[0, h, pl.ds(r0, rb), :] = ka.astype(BF16)
            va_ref[0, h, pl.ds(r0, rb), :] = va.astype(BF16)
        return c[rb - 1:rb, :]

    lax.fori_loop(0, s_len // rb, blk, jnp.zeros((1, GATE_W), F32))


def _fox_prep(fox_qkv, gates, bf_row, qg_row, kg_row, consts, b, s):
    rb = min(256, s)
    rq, rk, oq, ok = consts["fox_routes"]
    ltri = jnp.asarray(_tril_ones(rb), BF16)
    kern = functools.partial(_fox_prep_kernel, rb=rb)
    return pl.pallas_call(
        kern,
        grid=(b,),
        in_specs=[pl.BlockSpec((s, 3 * GROUP_W), lambda i: (i, 0)),
                  pl.BlockSpec((s, GATE_W), lambda i: (i, 0)),
                  _full_spec((1, GATE_W)), _full_spec((1, GROUP_W)), _full_spec((1, GROUP_W)),
                  _full_spec((GROUP_W, GROUP_W)), _full_spec((rb, rb)),
                  _full_spec(rq.shape), _full_spec(rk.shape), _full_spec(oq.shape), _full_spec(ok.shape)],
        out_specs=[pl.BlockSpec((1, N_HEADS, s, LANES), lambda i: (i, 0, 0, 0)),
                   pl.BlockSpec((1, N_HEADS, s, LANES), lambda i: (i, 0, 0, 0)),
                   pl.BlockSpec((1, N_HEADS, s, LANES), lambda i: (i, 0, 0, 0))],
        out_shape=[jax.ShapeDtypeStruct((b, N_HEADS, s, LANES), BF16)] * 3,
        compiler_params=_cparams(1),
        name="fox_prep",
    )(fox_qkv, gates, bf_row, qg_row, kg_row, consts["bd256"], ltri, rq, rk, oq, ok)


def _fox_attn_kernel(q_ref, k_ref, v_ref, og_ref, o_ref, s_scr, m_scr, acc_scr, *, blk):
    iq = pl.program_id(2)
    lane = lax.broadcasted_iota(jnp.int32, (blk, LANES), 1)
    row = lax.broadcasted_iota(jnp.int32, (blk, blk), 0)
    col = lax.broadcasted_iota(jnp.int32, (blk, blk), 1)

    def fold(s):
        m = s[:, 0:LANES]
        for c in range(1, blk // LANES):
            m = jnp.maximum(m, s[:, c * LANES:(c + 1) * LANES])
        return m

    def logits(hh, j, masked):
        r0 = pl.multiple_of(j * blk, blk)
        s = _dot_nt(q_ref[0, hh], k_ref[0, hh, pl.ds(r0, blk), :])
        if masked:
            s = jnp.where(col <= row, s, NEG_BIG)
        s_scr[hh, j] = s
        m_scr[hh] = jnp.maximum(m_scr[hh], fold(s))

    m_scr[...] = jnp.full(m_scr.shape, NEG_BIG, F32)

    def p1(j, _):
        for hh in range(2):
            logits(hh, j, False)
        return 0

    lax.fori_loop(0, iq, p1, 0)
    for hh in range(2):
        logits(hh, iq, True)

    row_max = [jnp.max(m_scr[hh], axis=-1, keepdims=True) for hh in range(2)]
    acc_scr[...] = jnp.zeros(acc_scr.shape, F32)

    def p2(j, _):
        r0 = pl.multiple_of(j * blk, blk)
        for hh in range(2):
            p = jnp.exp(s_scr[hh, j] - row_max[hh])
            acc_scr[hh] += _dot(p.astype(BF16), v_ref[0, hh, pl.ds(r0, blk), :])
        return 0

    lax.fori_loop(0, iq + 1, p2, 0)

    outs = []
    for hh in range(2):
        acc = acc_scr[hh]
        l = jnp.sum(jnp.where(lane == (1 - hh) * HEAD_DIM, acc, 0.0), axis=-1, keepdims=True)
        o = acc / l
        own = (lane // HEAD_DIM) == hh
        ms = jnp.sum(jnp.where(own, o * o, 0.0), axis=-1, keepdims=True) * (1.0 / HEAD_DIM)
        outs.append(o * lax.rsqrt(ms + EPS) * og_ref[...])
    o_ref[...] = jnp.where((lane // HEAD_DIM) == 0, outs[0], outs[1]).astype(BF16)


def _fox_attn(qa, ka, va, og_row, b, s):
    blk = min(FOX_BLOCK, s)
    nq = s // blk
    kern = functools.partial(_fox_attn_kernel, blk=blk)
    return pl.pallas_call(
        kern,
        grid=(b, 2, nq),
        in_specs=[pl.BlockSpec((1, 2, blk, LANES), lambda i, p, j: (i, p, j, 0)),
                  pl.BlockSpec((1, 2, s, LANES), lambda i, p, j: (i, p, 0, 0)),
                  pl.BlockSpec((1, 2, s, LANES), lambda i, p, j: (i, p, 0, 0)),
                  _full_spec((1, LANES))],
        out_specs=pl.BlockSpec((blk, LANES), lambda i, p, j: (i * nq + j, p)),
        out_shape=jax.ShapeDtypeStruct((b * s, GROUP_W), BF16),
        scratch_shapes=[pltpu.VMEM((2, nq, blk, blk), F32),
                        pltpu.VMEM((2, blk, LANES), F32),
                        pltpu.VMEM((2, blk, LANES), F32)],
        compiler_params=_cparams(3),
        name="fox_attn",
    )(qa, ka, va, og_row)


def _gdn_kernel(qkv_ref, z_ref, gates_ref, cw_ref, alog_ref, dt_ref, og_ref, bd_ref, eb_ref, eg_ref,
                eye_ref, strict_ref, causal_ref, ones_ref, ctri_ref,
                o_ref, xpad, q_scr, k_scr, v_scr, beta_scr, g_scr, o_scr, u_scr, s_scr,
                wq_scr, aqk_scr, kout_scr, *, rb):
    s_len = qkv_ref.shape[0]
    nblk = s_len // rb
    bd16 = bd_ref[...]
    bdf = bd16.astype(F32)

    xpad[0:8, :] = jnp.zeros((8, 3 * GROUP_W), F32)
    xpad[8:, :] = qkv_ref[...]

    def conv_blk(i, _):
        r0 = pl.multiple_of(i * rb, rb)

        gt = gates_ref[pl.ds(r0, rb), :]
        beta_scr[pl.ds(r0, rb), :] = _dot01_r(jax.nn.sigmoid(gt), eb_ref[...])
        g_s = -jnp.exp(alog_ref[...]) * _softplus(gt + dt_ref[...])
        g_rep = [_dot(t, eg_ref[...]).astype(BF16) for t in _split3(g_s)]

        def conv_silu(c0):
            xx = xpad[pl.ds(r0, rb + 8), c0:c0 + GROUP_W]
            y = cw_ref[0:1, c0:c0 + GROUP_W] * xx[5:5 + rb]
            for j in range(1, GDN_CONV):
                y = y + cw_ref[j:j + 1, c0:c0 + GROUP_W] * xx[5 + j:5 + j + rb]
            return _silu(y)

        q = conv_silu(0)
        q_ss = _seg_meansq(q, bd16, 1.0)
        k = conv_silu(GROUP_W)
        k_ss = _seg_meansq(k, bd16, 1.0)
        v_scr[pl.ds(r0, rb), :] = conv_silu(2 * GROUP_W)
        ct = ctri_ref[...]
        g_scr[pl.ds(r0, rb), :] = _dot(ct, g_rep[0]) + _dot(ct, g_rep[1]) + _dot(ct, g_rep[2])
        q_scr[pl.ds(r0, rb), :] = q * lax.rsqrt(q_ss + EPS) * (HEAD_DIM ** -0.5)
        k_scr[pl.ds(r0, rb), :] = k * lax.rsqrt(k_ss + EPS)
        return 0

    lax.fori_loop(0, nblk, conv_blk, 0)

    def bd(y16):
        return _tile_rows(y16, N_HEADS) * bd16

    eye = eye_ref[...]

    def wy_group(i, _):
        ns = [i * WY_GROUP + c for c in range(WY_GROUP)]
        r0s = [pl.multiple_of(n * CHUNK, CHUNK) for n in ns]
        q = [q_scr[pl.ds(r0, CHUNK), :] for r0 in r0s]
        k = [k_scr[pl.ds(r0, CHUNK), :] for r0 in r0s]
        beta = [beta_scr[pl.ds(r0, CHUNK), :] for r0 in r0s]
        g = [g_scr[pl.ds(r0, CHUNK), :] for r0 in r0s]
        kb = [kc * bc for kc, bc in zip(k, beta)]
        aa = [_dot_nt(jnp.concatenate([kbc.astype(BF16), qc.astype(BF16)], axis=0), bd(kc.astype(BF16)))
              for kbc, qc, kc in zip(kb, q, k)]
        g_row = [_dot01_l(ones_ref[...], gc * eye) for gc in g]
        decay = [jnp.exp(jnp.minimum(gc - grc, 0.0)) for gc, grc in zip(g, g_row)]
        m = [ac[0:CHUNK] * dc * strict_ref[...] for ac, dc in zip(aa, decay)]
        for r0, ac, dc in zip(r0s, aa, decay):
            aqk_scr[pl.ds(r0, CHUNK), :] = (ac[CHUNK:2 * CHUNK] * dc * causal_ref[...]).astype(BF16)
        pm = [eye - mc for mc in m]
        qm = [_dot(mc.astype(BF16), bd(mc.astype(BF16))) for mc in m]
        for it in range(5):
            q16 = [qc.astype(BF16) for qc in qm]
            if it < 4:
                r = [_dot(jnp.concatenate([pc.astype(BF16), qc], axis=0), bd(qc)) for pc, qc in zip(pm, q16)]
                pm = [pc + rc[0:CHUNK] for pc, rc in zip(pm, r)]
                qm = [rc[CHUNK:2 * CHUNK] for rc in r]
            else:
                pm = [pc + _dot(pc.astype(BF16), bd(qc)) for pc, qc in zip(pm, q16)]
        t16 = [pc.astype(BF16) for pc in pm]
        eg = [jnp.exp(gc) for gc in g]
        for c, (n, r0) in enumerate(zip(ns, r0s)):
            v = v_scr[pl.ds(r0, CHUNK), :]
            g_last = g_scr[pl.ds(r0 + CHUNK - 1, 1), :]
            wq_scr[n, 0:CHUNK, :] = _dot(t16[c], bd((kb[c] * eg[c]).astype(BF16))).astype(BF16)
            wq_scr[n, CHUNK:2 * CHUNK, :] = (q[c] * eg[c]).astype(BF16)
            u_scr[pl.ds(r0, CHUNK), :] = _dot(t16[c], bd((v * beta[c]).astype(BF16)))
            kout_scr[pl.ds(r0, CHUNK), :] = (k[c] * jnp.exp(g_last - g[c])).astype(BF16)
        return 0

    lax.fori_loop(0, s_len // (WY_GROUP * CHUNK), wy_group, 0)

    s_scr[...] = jnp.zeros((GROUP_W, GROUP_W), F32)

    def scan_chunk(n, _):
        r0 = pl.multiple_of(n * CHUNK, CHUNK)
        st = s_scr[...]
        wq = _dot(wq_scr[n], st.astype(BF16))
        v16 = (u_scr[pl.ds(r0, CHUNK), :] - wq[0:CHUNK]).astype(BF16)
        a_last = jnp.exp(g_scr[pl.ds(r0 + CHUNK - 1, 1), :])
        s_scr[...] = st * a_last + _dot_tn(kout_scr[pl.ds(r0, CHUNK), :], v16) * bdf
        o_scr[pl.ds(r0, CHUNK), :] = wq[CHUNK:2 * CHUNK] + _dot(aqk_scr[pl.ds(r0, CHUNK), :], bd(v16))
        return 0

    lax.fori_loop(0, s_len // CHUNK, scan_chunk, 0)

    def out_blk(i, _):
        r0 = pl.multiple_of(i * rb, rb)
        o = o_scr[pl.ds(r0, rb), :]
        on = o * lax.rsqrt(_seg_meansq(o, bd16, HEAD_DIM) + EPS) * og_ref[...]
        o_ref[pl.ds(r0, rb), :] = (on * _silu(z_ref[pl.ds(r0, rb), :])).astype(BF16)
        return 0

    lax.fori_loop(0, nblk, out_blk, 0)


def _gdn(gdn_qkv, gdn_z, gates, cw, alog_row, dt_row, og_row, consts, b, s):
    rb = min(256, s)
    eye, strict, causal = consts["chunk_masks"]
    kern = functools.partial(_gdn_kernel, rb=rb)
    row = lambda w: pl.BlockSpec((s, w), lambda i: (i, 0))
    return pl.pallas_call(
        kern,
        grid=(b,),
        in_specs=[row(3 * GROUP_W), row(GROUP_W), row(GATE_W),
                  _full_spec((8, 3 * GROUP_W)), _full_spec((1, GATE_W)), _full_spec((1, GATE_W)),
                  _full_spec((1, GROUP_W)), _full_spec((GROUP_W, GROUP_W)),
                  _full_spec((GATE_W, GROUP_W)), _full_spec((GATE_W, GROUP_W)),
                  _full_spec((CHUNK, GROUP_W)), _full_spec((CHUNK, GROUP_W)), _full_spec((CHUNK, GROUP_W)),
                  _full_spec((CHUNK, CHUNK)), _full_spec((rb, rb))],
        out_specs=row(GROUP_W),
        out_shape=jax.ShapeDtypeStruct((b * s, GROUP_W), BF16),
        scratch_shapes=[pltpu.VMEM((s + 8, 3 * GROUP_W), F32)]
                       + [pltpu.VMEM((s, GROUP_W), F32) for _ in range(7)]
                       + [pltpu.VMEM((GROUP_W, GROUP_W), F32),
                          pltpu.VMEM((s // CHUNK, 2 * CHUNK, GROUP_W), BF16),
                          pltpu.VMEM((s, GROUP_W), BF16), pltpu.VMEM((s, GROUP_W), BF16)],
        compiler_params=_cparams(1),
        name="gdn",
    )(gdn_qkv, gdn_z, gates, cw, alog_row, dt_row, og_row, consts["bd256"], consts["expand_b"], consts["expand_g"],
      eye, strict, causal, consts["ones64"],
      jnp.asarray(_tril_ones(rb) * _block_mask(rb, CHUNK, rb, CHUNK), BF16))


def _la_kernel(*refs, variant, rb):
    if variant == "hgrn2":
        (x_ref, la_ref, l1_ref, oml_ref, og_ref, bdv_ref, bdk_ref, eye_ref, lvl_ref, ltri_ref,
         o_ref, q_scr, k_scr, a_scr, v_scr, gate_scr, gc_scr, o_scr, st_scr) = refs
        kw, dk = GROUP_W, HEAD_DIM
    else:
        (x_ref, gates_ref, wgk_ref, bgk_ref, og_ref, bdv_ref, bdk_ref, eye_ref, lvl_ref, ltri_ref,
         o_ref, q_scr, k_scr, a_scr, v_scr, gate_scr, gc_scr, o_scr, st_scr) = refs
        kw, dk = GLA_KW, GLA_DK
    s_len = x_ref.shape[0]
    nblk = s_len // rb
    bdv16 = bdv_ref[...]
    bdk16 = bdk_ref[...]
    bdkf = bdk16.astype(F32)

    def prep_blk(i, _):
        r0 = pl.multiple_of(i * rb, rb)
        x = x_ref[pl.ds(r0, rb), :]
        if variant == "hgrn2":
            f_logit = x[:, GROUP_W:2 * GROUP_W]
            q_scr[pl.ds(r0, rb), :] = _silu(x[:, 0:GROUP_W]) * (dk ** -0.5)
            a = la_ref[...]
            bterm = l1_ref[...] + _log_sigmoid(f_logit)
            amax = jnp.maximum(a, bterm)
            a_scr[pl.ds(r0, rb), :] = amax + jnp.log1p(jnp.exp(-jnp.abs(a - bterm)))
            k_scr[pl.ds(r0, rb), :] = oml_ref[...] * jax.nn.sigmoid(-f_logit)
            v_scr[pl.ds(r0, rb), :] = x[:, 2 * GROUP_W:3 * GROUP_W]
            gate_scr[pl.ds(r0, rb), :] = x[:, 3 * GROUP_W:4 * GROUP_W]
        else:
            q_scr[pl.ds(r0, rb), :] = x[:, 0:kw] * (dk ** -0.5)
            k_scr[pl.ds(r0, rb), :] = x[:, kw:2 * kw]
            v_scr[pl.ds(r0, rb), :] = x[:, 2 * kw:2 * kw + GROUP_W]
            gate_scr[pl.ds(r0, rb), :] = x[:, 2 * kw + GROUP_W:2 * kw + 2 * GROUP_W]
            lr = _dot(gates_ref[pl.ds(r0, rb), :].astype(BF16), wgk_ref[...]) + bgk_ref[...]
            a_scr[pl.ds(r0, rb), :] = _log_sigmoid(lr) * (1.0 / GLA_NORM)
        return 0

    lax.fori_loop(0, nblk, prep_blk, 0)

    st_scr[...] = jnp.zeros((GROUP_W, kw), F32)
    sub = lax.broadcasted_iota(jnp.int32, (8, kw), 0)

    def level_ref(m, slot):
        def brow(r):
            return jnp.broadcast_to(gc_scr[slot, r:r + 1, :], (8, kw))
        pieces = []
        for a in range(CHUNK // 8):
            if 2 * m >= 8:
                pieces.append(brow((8 * a) // (2 * m) * (2 * m) + m - 1))
            elif m == 2:
                pieces.append(jnp.where(sub < 4, brow(8 * a + 1), brow(8 * a + 5)))
            else:
                p = jnp.where(sub < 2, brow(8 * a), brow(8 * a + 2))
                p = jnp.where(sub < 4, p, jnp.where(sub < 6, brow(8 * a + 4), brow(8 * a + 6)))
                pieces.append(p)
        return jnp.concatenate(pieces, axis=0)

    def chunk_group(i, _):
        slots = range(WY_GROUP)
        r0s = [pl.multiple_of((i * WY_GROUP + c) * CHUNK, CHUNK) for c in slots]
        q = [q_scr[pl.ds(r0, CHUNK), :] for r0 in r0s]
        k = [k_scr[pl.ds(r0, CHUNK), :] for r0 in r0s]
        v16 = [v_scr[pl.ds(r0, CHUNK), :].astype(BF16) for r0 in r0s]
        g = [_dot01_l(ltri_ref[...], a_scr[pl.ds(r0, CHUNK), :]) for r0 in r0s]
        for c in slots:
            gc_scr[c] = g[c]
        a_in = [_dot_nt(q[c].astype(BF16), _tile_rows(k[c].astype(BF16), N_HEADS) * bdk16) * eye_ref[...]
                for c in slots]
        for li, m in enumerate(LEVELS):
            for c in slots:
                e = jnp.exp(-jnp.abs(g[c] - level_ref(m, c)))
                qe = (q[c] * e).astype(BF16)
                ke = (k[c] * e).astype(BF16)
                a_in[c] = a_in[c] + _dot_nt(qe, _tile_rows(ke, N_HEADS) * bdk16) * lvl_ref[li]
        o = [_dot(a_in[c].astype(BF16), _tile_rows(v16[c], N_HEADS) * bdv16) for c in slots]
        ds = [_dot_tn(v16[c], (k[c] * jnp.exp(g[c][CHUNK - 1:CHUNK, :] - g[c])).astype(BF16)) * bdkf for c in slots]
        q_in = [(q[c] * jnp.exp(g[c])).astype(BF16) for c in slots]
        st = st_scr[...]
        for c in slots:
            o_scr[pl.ds(r0s[c], CHUNK), :] = o[c] + _dot_nt(q_in[c], st.astype(BF16))
            st = st * jnp.exp(g[c][CHUNK - 1:CHUNK, :]) + ds[c]
        st_scr[...] = st
        return 0

    lax.fori_loop(0, s_len // (WY_GROUP * CHUNK), chunk_group, 0)

    def out_blk(i, _):
        r0 = pl.multiple_of(i * rb, rb)
        o = o_scr[pl.ds(r0, rb), :]
        on = o * lax.rsqrt(_seg_meansq(o, bdv16, HEAD_DIM) + EPS) * og_ref[...]
        o_ref[pl.ds(r0, rb), :] = (on * _silu(gate_scr[pl.ds(r0, rb), :])).astype(BF16)
        return 0

    lax.fori_loop(0, nblk, out_blk, 0)


def _la(variant, x, extra, og_row, consts, b, s):
    rb = min(256, s)
    kw = GROUP_W if variant == "hgrn2" else GLA_KW
    xw = 4 * GROUP_W if variant == "hgrn2" else 3 * GROUP_W
    kern = functools.partial(_la_kernel, variant=variant, rb=rb)
    row = lambda w: pl.BlockSpec((s, w), lambda i: (i, 0))
    if variant == "hgrn2":
        in_specs = [row(xw)] + [_full_spec((1, GROUP_W))] * 3
        bdk = consts["bd256"]
    else:
        in_specs = [row(xw), row(GATE_W), _full_spec((GATE_W, GLA_KW)), _full_spec((1, GLA_KW))]
        bdk = consts["bdk_gla"]
    in_specs += [_full_spec((1, GROUP_W)), _full_spec((GROUP_W, GROUP_W)), _full_spec((GROUP_W, kw)),
                 _full_spec((CHUNK, GROUP_W)), _full_spec((len(LEVELS), CHUNK, GROUP_W)), _full_spec((CHUNK, CHUNK))]
    return pl.pallas_call(
        kern,
        grid=(b,),
        in_specs=in_specs,
        out_specs=row(GROUP_W),
        out_shape=jax.ShapeDtypeStruct((b * s, GROUP_W), BF16),
        scratch_shapes=[pltpu.VMEM((s, kw), F32) for _ in range(3)]
                       + [pltpu.VMEM((s, GROUP_W), F32) for _ in range(2)]
                       + [pltpu.VMEM((WY_GROUP, CHUNK, kw), F32), pltpu.VMEM((s, GROUP_W), F32),
                          pltpu.VMEM((GROUP_W, kw), F32)],
        compiler_params=_cparams(1),
        name="la_" + variant,
    )(x, *extra, og_row, consts["bd256"], bdk, consts["chunk_masks"][0], consts["level_masks"], consts["ltri64"])


FFN_TAIL = 16


def _ffn_kernel(x_ref, oa_ref, ob_ref, oc_ref, od_ref, wo_ref, g2_ref, wup_ref, ctap_ref, wd_ref,
                out_ref, hext, u_scr, act_scr, *, tm):
    it = pl.program_id(1)
    x1 = x_ref[...]
    for kk, o_ref in enumerate((oa_ref, ob_ref, oc_ref, od_ref)):
        x1 = x1 + _dot(o_ref[...], wo_ref[kk])
    out_ref[...] = x1
    ms = jnp.mean(x1 * x1, axis=-1, keepdims=True)
    h2 = (x1 * lax.rsqrt(ms + EPS) * g2_ref[...]).astype(BF16)

    @pl.when(it == 0)
    def _():
        hext[0:FFN_TAIL, :] = jnp.zeros((FFN_TAIL, D_MODEL), BF16)

    hext[FFN_TAIL:, :] = h2

    def conv(col0, slot):
        u_scr[slot] = _dot(hext[...], wup_ref[:, col0:col0 + FF_CHUNK])
        c = ctap_ref[:, col0:col0 + FF_CHUNK]
        y = c[0:1, :] * u_scr[slot, FFN_TAIL - 2:FFN_TAIL - 2 + tm, :]
        y = y + c[1:2, :] * u_scr[slot, FFN_TAIL - 1:FFN_TAIL - 1 + tm, :]
        y = y + c[2:3, :] * u_scr[slot, FFN_TAIL:FFN_TAIL + tm, :]
        return y + c[3:4, :]

    for j in range(D_FF // FF_CHUNK):
        slot = 2 * (j % 2)
        gate = conv(j * FF_CHUNK, slot)
        up = conv(D_FF + j * FF_CHUNK, slot + 1)
        act_scr[:, j * FF_CHUNK:(j + 1) * FF_CHUNK] = (_silu(gate) * up).astype(BF16)
    out_ref[...] += _dot(act_scr[...], wd_ref[...])
    hext[0:FFN_TAIL, :] = hext[tm:tm + FFN_TAIL, :]


def _ffn(x2d, outs, wo, g2, wup, ctap, wd, b, s):
    tm = min(ROW_TILE, s)
    nt = s // tm
    kern = functools.partial(_ffn_kernel, tm=tm)
    row = lambda w: pl.BlockSpec((tm, w), lambda i, j: (i * nt + j, 0))
    return pl.pallas_call(
        kern,
        grid=(b, nt),
        in_specs=[row(D_MODEL)] + [row(GROUP_W)] * 4
                 + [_full_spec((4, GROUP_W, D_MODEL)), _full_spec((1, D_MODEL)),
                    _full_spec((D_MODEL, 2 * D_FF)), _full_spec((8, 2 * D_FF)), _full_spec((D_FF, D_MODEL))],
        out_specs=row(D_MODEL),
        out_shape=jax.ShapeDtypeStruct(x2d.shape, F32),
        scratch_shapes=[pltpu.VMEM((tm + FFN_TAIL, D_MODEL), BF16),
                        pltpu.VMEM((4, tm + FFN_TAIL, FF_CHUNK), F32),
                        pltpu.VMEM((tm, D_FF), BF16)],
        compiler_params=_cparams(2),
        name="outproj_ffn",
    )(x2d, *outs, wo, g2, wup, ctap, wd)


def _constants():
    eye, strict, causal = _chunk_masks()
    return {
        "bd256": jnp.asarray(_block_mask(GROUP_W, HEAD_DIM, GROUP_W, HEAD_DIM), BF16),
        "bdk_gla": jnp.asarray(_block_mask(GROUP_W, HEAD_DIM, GLA_KW, GLA_DK), BF16),
        "expand_b": jnp.asarray(_expand_mat(GATE_GDN_B), BF16),
        "expand_g": jnp.asarray(_expand_mat(GATE_GDN_A), BF16),
        "chunk_masks": (jnp.asarray(eye), jnp.asarray(strict), jnp.asarray(causal)),
        "level_masks": jnp.asarray(_level_masks()),
        "ones64": jnp.ones((CHUNK, CHUNK), BF16),
        "ltri64": jnp.asarray(_tril_ones(CHUNK), BF16),
        "fox_routes": tuple(jnp.asarray(a, BF16) if a.shape[1] > 1 else jnp.asarray(a) for a in _fox_routes()),
    }


def _w_in_moves():
    sizes = (3 * GROUP_W, N_HEADS, 3 * GROUP_W, N_HEADS, N_HEADS, GROUP_W,
             GROUP_W, GROUP_W, GROUP_W, GROUP_W, 2 * GLA_KW, GROUP_W, GLA_RANK, GROUP_W)
    names = ("fox_qkv", "fox_f", "gdn_qkv", "gdn_b", "gdn_a", "gdn_z", "hg_q", "hg_f", "hg_i", "hg_g",
             "gla_qk", "gla_v", "gla_gk", "gla_g")
    src = dict(zip(names, np.concatenate([[0], np.cumsum(sizes)[:-1]])))
    wid = dict(zip(names, sizes))
    order = ("fox_qkv", "gdn_qkv", "gdn_z", "hg_q", "hg_f", "hg_i", "hg_g", "gla_qk", "gla_v", "gla_g",
             "fox_f", "gdn_b", "gdn_a", "gla_gk")
    moves, dst = [], 0
    for nm in order:
        moves.append((int(src[nm]), dst, wid[nm]))
        dst += wid[nm]
    return moves, dst


def _regroup_kernel(w_ref, o_ref):
    moves, used = _w_in_moves()
    x = w_ref[0]
    for src, dst, wd in moves:
        o_ref[0, :, dst:dst + wd] = x[:, src:src + wd].astype(BF16)
    o_ref[0, :, used:IN_COLS_PAD] = jnp.zeros((x.shape[0], IN_COLS_PAD - used), BF16)


def _regroup_w_in(w_in):
    depth, d, cols = w_in.shape
    tr = 256
    return pl.pallas_call(
        _regroup_kernel,
        grid=(depth, d // tr),
        in_specs=[pl.BlockSpec((1, tr, cols), lambda l, i: (l, i, 0))],
        out_specs=pl.BlockSpec((1, tr, IN_COLS_PAD), lambda l, i: (l, i, 0)),
        out_shape=jax.ShapeDtypeStruct((depth, d, IN_COLS_PAD), BF16),
        compiler_params=_cparams(2),
        name="regroup_w_in",
    )(w_in)


def _lane_row(vals, lane0, width):
    return jnp.zeros((1, width), F32).at[0, lane0:lane0 + vals.shape[0]].set(vals.astype(F32))


def _tile_heads(g):
    return jnp.tile(g.astype(F32), N_HEADS)[None, :]


def kernel(x, norm1_g, w_in, fox_qn_g, fox_kn_g, fox_b_f, fox_on_g, gdn_conv_w, gdn_a_log, gdn_dt_bias, gdn_on_g,
           hg_lb, hg_on_g, gla_w_gk, gla_b_gk, gla_on_g, w_out, norm2_g, w_up, ffn_conv_w, ffn_conv_b, w_down):
    b, s, d = x.shape
    depth = w_in.shape[0]
    consts = _constants()

    cs = jnp.cumsum(jax.nn.softmax(hg_lb.astype(F32), axis=0), axis=0)
    lower = cs - cs[0:1]

    w_in_all = _regroup_w_in(w_in)
    x2d = x.reshape(b * s, d)
    for l in range(depth):
        fox_qkv, gdn_qkv, gdn_z, hg, gla, gates = _inproj(x2d, norm1_g[l][None, :], w_in_all, l)

        qa, ka, va = _fox_prep(fox_qkv, gates, _lane_row(fox_b_f[l], GATE_FOX_F, GATE_W),
                               _tile_heads(fox_qn_g[l]), _tile_heads(fox_kn_g[l]), consts, b, s)
        o_a = _fox_attn(qa, ka, va, jnp.tile(fox_on_g[l].astype(F32), 2)[None, :], b, s)

        cw = jnp.zeros((8, 3 * GROUP_W), F32).at[0:GDN_CONV].set(gdn_conv_w[l])
        o_b = _gdn(gdn_qkv, gdn_z, gates, cw, _lane_row(gdn_a_log[l], GATE_GDN_A, GATE_W),
                   _lane_row(gdn_dt_bias[l], GATE_GDN_A, GATE_W), _tile_heads(gdn_on_g[l]), consts, b, s)

        lb = lower[l][None, :]
        o_c = _la("hgrn2", hg, (jnp.log(lb), jnp.log1p(-lb), 1.0 - lb), _tile_heads(hg_on_g[l]), consts, b, s)

        wgk = jnp.zeros((GATE_W, GLA_KW), F32).at[GATE_GLA_GK:GATE_GLA_GK + GLA_RANK].set(gla_w_gk[l]).astype(BF16)
        o_d = _la("gla", gla, (gates, wgk, gla_b_gk[l][None, :].astype(F32)), _tile_heads(gla_on_g[l]), consts, b, s)

        wo = w_out[l].reshape(4, GROUP_W, d).astype(BF16)
        taps = jnp.concatenate([ffn_conv_w[l], ffn_conv_b[l][None, :],
                                jnp.zeros((8 - FFN_CONV - 1, 2 * D_FF), F32)], axis=0)
        x2d = _ffn(x2d, (o_a, o_b, o_c, o_d), wo, norm2_g[l][None, :], w_up[l].astype(BF16), taps,
                   w_down[l].astype(BF16), b, s)
    return x2d.reshape(b, s, d)
```

```python
import functools

import numpy as np
import jax
import jax.numpy as jnp
from jax import lax
from jax.experimental import pallas as pl
from jax.experimental.pallas import tpu as pltpu

F32 = jnp.float32
BF16 = jnp.bfloat16

D_MODEL = 1024
N_HEADS = 4
HEAD_DIM = 64
GROUP_W = N_HEADS * HEAD_DIM
GLA_DK = 32
GLA_KW = N_HEADS * GLA_DK
GLA_RANK = 16
GLA_NORM = 16.0
GDN_CONV = 4
D_FF = 2816
FFN_CONV = 3
EPS = 1e-6
CHUNK = 64
WY_GROUP = 4

LANES = 128
GATE_W = LANES
GATE_FOX_F, GATE_GDN_B, GATE_GDN_A, GATE_GLA_GK = 0, 4, 8, 12

IN_SEGS = (3 * GROUP_W, 3 * GROUP_W, GROUP_W, 4 * GROUP_W, 3 * GROUP_W, GATE_W)
IN_COLS_PAD = sum(IN_SEGS)

ROW_TILE = 512
FF_CHUNK = 256
FOX_BLOCK = 512
NEG_BIG = -1e30

VMEM_LIMIT = 56 * 1024 * 1024


def _dot(a, b):
    return jnp.dot(a, b, preferred_element_type=F32)


def _dot_nt(a, b):
    return lax.dot_general(a, b, (((1,), (1,)), ((), ())), preferred_element_type=F32)


def _dot_tn(a, b):
    return lax.dot_general(a, b, (((0,), (0,)), ((), ())), preferred_element_type=F32)


def _split3(x):
    hi = x.astype(BF16)
    r1 = x - hi.astype(F32)
    mid = r1.astype(BF16)
    lo = (r1 - mid.astype(F32)).astype(BF16)
    return hi, mid, lo


def _dot01_l(m01, x):
    hi, mid, lo = _split3(x)
    return _dot(m01, hi) + _dot(m01, mid) + _dot(m01, lo)


def _dot01_r(x, m01):
    hi, mid, lo = _split3(x)
    return _dot(hi, m01) + _dot(mid, m01) + _dot(lo, m01)


def _seg_meansq(x, bd16, width):
    sq = x * x
    hi = sq.astype(BF16)
    lo = (sq - hi.astype(F32)).astype(BF16)
    return (_dot(hi, bd16) + _dot(lo, bd16)) * (1.0 / width)


def _log_sigmoid(x):
    return jnp.minimum(x, 0.0) - jnp.log1p(jnp.exp(-jnp.abs(x)))


def _softplus(x):
    return jnp.maximum(x, 0.0) + jnp.log1p(jnp.exp(-jnp.abs(x)))


def _silu(x):
    return x * jax.nn.sigmoid(x)


def _tile_rows(x, n):
    return jnp.concatenate([x] * n, axis=0)


def _interleave(*gens):
    live = list(gens)
    while live:
        for gen in list(live):
            try:
                next(gen)
            except StopIteration:
                live.remove(gen)


def _full_spec(shape):
    nd = len(shape)
    return pl.BlockSpec(shape, lambda *_: (0,) * nd, pipeline_mode=pl.Buffered(1))


def _cparams(n_axes):
    return pltpu.CompilerParams(dimension_semantics=("arbitrary",) * n_axes, vmem_limit_bytes=VMEM_LIMIT)


def _block_mask(rows, row_blk, cols, col_blk):
    r = np.arange(rows)[:, None] // row_blk
    c = np.arange(cols)[None, :] // col_blk
    return (r == c).astype(np.float32)


def _tril_ones(n):
    return np.tril(np.ones((n, n), np.float32))


def _expand_mat(lane0):
    m = np.zeros((GATE_W, GROUP_W), np.float32)
    for h in range(N_HEADS):
        m[lane0 + h, h * HEAD_DIM:(h + 1) * HEAD_DIM] = 1.0
    return m


def _chunk_masks():
    r = np.arange(CHUNK)[:, None]
    s = np.arange(GROUP_W)[None, :] % CHUNK
    return ((r == s).astype(np.float32), (r > s).astype(np.float32), (r >= s).astype(np.float32))


LEVELS = (1, 2, 4, 8, 16, 32)


def _level_masks():
    r = np.arange(CHUNK)[:, None]
    s = np.arange(GROUP_W)[None, :] % CHUNK
    out = []
    for m in LEVELS:
        same = (r // (2 * m)) == (s // (2 * m))
        out.append((same & ((r % (2 * m)) >= m) & ((s % (2 * m)) < m)).astype(np.float32))
    return np.stack(out)


def _fox_routes():
    rq = np.zeros((3 * GATE_W, N_HEADS * LANES), np.float32)
    rk = np.zeros((3 * GATE_W, N_HEADS * LANES), np.float32)
    oq = np.zeros((N_HEADS, 1, LANES), np.float32)
    ok = np.zeros((N_HEADS, 1, LANES), np.float32)
    for h in range(N_HEADS):
        base = (1 - h % 2) * HEAD_DIM
        for j in range(3):
            rq[j * GATE_W + GATE_FOX_F + h, h * LANES + base + j] = 1.0
            rk[j * GATE_W + GATE_FOX_F + h, h * LANES + base + 3 + j] = -1.0
            oq[h, 0, base + 3 + j] = 1.0
            ok[h, 0, base + j] = 1.0
    return rq, rk, oq, ok


def _inproj_kernel(x_ref, g_ref, w_ref, *out_refs):
    x = x_ref[...]
    ms = jnp.mean(x * x, axis=-1, keepdims=True)
    h = (x * lax.rsqrt(ms + EPS) * g_ref[...]).astype(BF16)
    off = 0
    for ref, width in zip(out_refs, IN_SEGS):
        step = min(width, 256)
        for c in range(0, width, step):
            ref[:, c:c + step] = _dot(h, w_ref[:, off + c:off + c + step])
        off += width


def _inproj(x2d, g, w_all, layer):
    t = x2d.shape[0]
    tm = min(ROW_TILE, t)
    return pl.pallas_call(
        _inproj_kernel,
        grid=(t // tm,),
        in_specs=[pl.BlockSpec((tm, D_MODEL), lambda i: (i, 0)),
                  _full_spec((1, D_MODEL)),
                  pl.BlockSpec((None, D_MODEL, IN_COLS_PAD), lambda i: (layer, 0, 0), pipeline_mode=pl.Buffered(1))],
        out_specs=[pl.BlockSpec((tm, wd), lambda i: (i, 0)) for wd in IN_SEGS],
        out_shape=[jax.ShapeDtypeStruct((t, wd), F32) for wd in IN_SEGS],
        compiler_params=_cparams(1),
        name="inproj",
    )(x2d, g, w_all)


def _fox_prep_kernel(qkv_ref, gates_ref, bf_ref, qg_ref, kg_ref, bd_ref, ltri_ref, rq_ref, rk_ref, oq_ref, ok_ref,
                     qa_ref, ka_ref, va_ref, *, rb):
    s_len = qkv_ref.shape[0]
    lane = lax.broadcasted_iota(jnp.int32, (rb, LANES), 1)

    def blk(i, carry):
        r0 = pl.multiple_of(i * rb, rb)
        logf = _log_sigmoid(gates_ref[pl.ds(r0, rb), :] + bf_ref[...])
        c = _dot01_l(ltri_ref[...], logf) + carry
        c3 = jnp.concatenate(_split3(c), axis=1)
        qkv = qkv_ref[pl.ds(r0, rb), :]
        q = qkv[:, 0:GROUP_W]
        k = qkv[:, GROUP_W:2 * GROUP_W]
        v = qkv[:, 2 * GROUP_W:3 * GROUP_W]
        bd = bd_ref[...]
        qn = q * lax.rsqrt(_seg_meansq(q, bd, HEAD_DIM) + EPS) * qg_ref[...] * (HEAD_DIM ** -0.5)
        kn = k * lax.rsqrt(_seg_meansq(k, bd, HEAD_DIM) + EPS) * kg_ref[...]
        q_route = _dot(c3, rq_ref[...])
        k_route = _dot(c3, rk_ref[...])
        for h in range(N_HEADS):
            p = h // 2
            own = (lane // HEAD_DIM) == (h % 2)
            qa = jnp.where(own, qn[:, p * LANES:(p + 1) * LANES], q_route[:, h * LANES:(h + 1) * LANES] + oq_ref[h])
            ka = jnp.where(own, kn[:, p * LANES:(p + 1) * LANES], k_route[:, h * LANES:(h + 1) * LANES] + ok_ref[h])
            va = jnp.where(own, v[:, p * LANES:(p + 1) * LANES], jnp.where(lane == (1 - h % 2) * HEAD_DIM, 1.0, 0.0))
            qa_ref[0, h, pl.ds(r0, rb), :] = qa.astype(BF16)
            ka_ref[0, h, pl.ds(r0, rb), :] = ka.astype(BF16)
            va_ref[0, h, pl.ds(r0, rb), :] = va.astype(BF16)
        return c[rb - 1:rb, :]

    lax.fori_loop(0, s_len // rb, blk, jnp.zeros((1, GATE_W), F32))


def _fox_prep(fox_qkv, gates, bf_row, qg_row, kg_row, consts, b, s):
    rb = min(256, s)
    rq, rk, oq, ok = consts["fox_routes"]
    ltri = jnp.asarray(_tril_ones(rb), BF16)
    kern = functools.partial(_fox_prep_kernel, rb=rb)
    return pl.pallas_call(
        kern,
        grid=(b,),
        in_specs=[pl.BlockSpec((s, 3 * GROUP_W), lambda i: (i, 0)),
                  pl.BlockSpec((s, GATE_W), lambda i: (i, 0)),
                  _full_spec((1, GATE_W)), _full_spec((1, GROUP_W)), _full_spec((1, GROUP_W)),
                  _full_spec((GROUP_W, GROUP_W)), _full_spec((rb, rb)),
                  _full_spec(rq.shape), _full_spec(rk.shape), _full_spec(oq.shape), _full_spec(ok.shape)],
        out_specs=[pl.BlockSpec((1, N_HEADS, s, LANES), lambda i: (i, 0, 0, 0)),
                   pl.BlockSpec((1, N_HEADS, s, LANES), lambda i: (i, 0, 0, 0)),
                   pl.BlockSpec((1, N_HEADS, s, LANES), lambda i: (i, 0, 0, 0))],
        out_shape=[jax.ShapeDtypeStruct((b, N_HEADS, s, LANES), BF16)] * 3,
        compiler_params=_cparams(1),
        name="fox_prep",
    )(fox_qkv, gates, bf_row, qg_row, kg_row, consts["bd256"], ltri, rq, rk, oq, ok)


def _fox_attn_kernel(q_ref, k_ref, v_ref, og_ref, o_ref, s_scr, m_scr, acc_scr, *, blk, nq):
    lane = lax.broadcasted_iota(jnp.int32, (blk, LANES), 1)
    row = lax.broadcasted_iota(jnp.int32, (blk, blk), 0)
    col = lax.broadcasted_iota(jnp.int32, (blk, blk), 1)

    def fold(s):
        m = s[:, 0:LANES]
        for c in range(1, blk // LANES):
            m = jnp.maximum(m, s[:, c * LANES:(c + 1) * LANES])
        return m

    def logit_stages(i):
        slot = i % 2
        for j in range(i + 1):
            for hh in range(2):
                s = _dot_nt(q_ref[0, hh, i * blk:(i + 1) * blk, :], k_ref[0, hh, j * blk:(j + 1) * blk, :])
                if j == i:
                    s = jnp.where(col <= row, s, NEG_BIG)
                s_scr[slot, hh, j] = s
                m_scr[slot, hh] = fold(s) if j == 0 else jnp.maximum(m_scr[slot, hh], fold(s))
            yield

    def value_stages(i):
        slot = i % 2
        row_max = [jnp.max(m_scr[slot, hh], axis=-1, keepdims=True) for hh in range(2)]
        for j in range(i + 1):
            for hh in range(2):
                p = jnp.exp(s_scr[slot, hh, j] - row_max[hh])
                pv = _dot(p.astype(BF16), v_ref[0, hh, j * blk:(j + 1) * blk, :])
                acc_scr[slot, hh] = pv if j == 0 else acc_scr[slot, hh] + pv
            yield
        outs = []
        for hh in range(2):
            acc = acc_scr[slot, hh]
            l = jnp.sum(jnp.where(lane == (1 - hh) * HEAD_DIM, acc, 0.0), axis=-1, keepdims=True)
            o = acc / l
            own = (lane // HEAD_DIM) == hh
            ms = jnp.sum(jnp.where(own, o * o, 0.0), axis=-1, keepdims=True) * (1.0 / HEAD_DIM)
            outs.append(o * lax.rsqrt(ms + EPS) * og_ref[...])
        o_ref[i * blk:(i + 1) * blk, :] = jnp.where((lane // HEAD_DIM) == 0, outs[0], outs[1]).astype(BF16)

    _interleave(logit_stages(0))
    for i in range(nq):
        if i + 1 < nq:
            _interleave(value_stages(i), logit_stages(i + 1))
        else:
            _interleave(value_stages(i))


def _fox_attn(qa, ka, va, og_row, b, s):
    blk = min(FOX_BLOCK, s)
    nq = s // blk
    kern = functools.partial(_fox_attn_kernel, blk=blk, nq=nq)
    head_pair = pl.BlockSpec((1, 2, s, LANES), lambda i, p: (i, p, 0, 0))
    return pl.pallas_call(
        kern,
        grid=(b, 2),
        in_specs=[head_pair, head_pair, head_pair, _full_spec((1, LANES))],
        out_specs=pl.BlockSpec((s, LANES), lambda i, p: (i, p)),
        out_shape=jax.ShapeDtypeStruct((b * s, GROUP_W), BF16),
        scratch_shapes=[pltpu.VMEM((2, 2, nq, blk, blk), F32),
                        pltpu.VMEM((2, 2, blk, LANES), F32),
                        pltpu.VMEM((2, 2, blk, LANES), F32)],
        compiler_params=_cparams(2),
        name="fox_attn",
    )(qa, ka, va, og_row)


def _gdn_kernel(qkv_ref, z_ref, gates_ref, cw_ref, alog_ref, dt_ref, og_ref, bd_ref, eb_ref, eg_ref,
                eye_ref, strict_ref, causal_ref, ones_ref, ctri_ref,
                o_ref, xpad, q_scr, k_scr, v_scr, beta_scr, g_scr, o_scr, u_scr, s_scr,
                wq_scr, aqk_scr, kout_scr, *, rb):
    s_len = qkv_ref.shape[0]
    nblk = s_len // rb
    bd16 = bd_ref[...]
    bdf = bd16.astype(F32)

    xpad[0:8, :] = jnp.zeros((8, 3 * GROUP_W), F32)
    xpad[8:, :] = qkv_ref[...]

    def conv_blk(i, _):
        r0 = pl.multiple_of(i * rb, rb)

        gt = gates_ref[pl.ds(r0, rb), :]
        beta_scr[pl.ds(r0, rb), :] = _dot01_r(jax.nn.sigmoid(gt), eb_ref[...])
        g_s = -jnp.exp(alog_ref[...]) * _softplus(gt + dt_ref[...])
        g_rep = [_dot(t, eg_ref[...]).astype(BF16) for t in _split3(g_s)]

        def conv_silu(c0):
            xx = xpad[pl.ds(r0, rb + 8), c0:c0 + GROUP_W]
            y = cw_ref[0:1, c0:c0 + GROUP_W] * xx[5:5 + rb]
            for j in range(1, GDN_CONV):
                y = y + cw_ref[j:j + 1, c0:c0 + GROUP_W] * xx[5 + j:5 + j + rb]
            return _silu(y)

        q = conv_silu(0)
        q_ss = _seg_meansq(q, bd16, 1.0)
        k = conv_silu(GROUP_W)
        k_ss = _seg_meansq(k, bd16, 1.0)
        v_scr[pl.ds(r0, rb), :] = conv_silu(2 * GROUP_W)
        ct = ctri_ref[...]
        g_scr[pl.ds(r0, rb), :] = _dot(ct, g_rep[0]) + _dot(ct, g_rep[1]) + _dot(ct, g_rep[2])
        q_scr[pl.ds(r0, rb), :] = q * lax.rsqrt(q_ss + EPS) * (HEAD_DIM ** -0.5)
        k_scr[pl.ds(r0, rb), :] = k * lax.rsqrt(k_ss + EPS)
        return 0

    lax.fori_loop(0, nblk, conv_blk, 0)

    def bd(y16):
        return _tile_rows(y16, N_HEADS) * bd16

    eye = eye_ref[...]

    def wy_stages(i):
        ns = [i * WY_GROUP + c for c in range(WY_GROUP)]
        r0s = [pl.multiple_of(n * CHUNK, CHUNK) for n in ns]
        q = [q_scr[pl.ds(r0, CHUNK), :] for r0 in r0s]
        k = [k_scr[pl.ds(r0, CHUNK), :] for r0 in r0s]
        beta = [beta_scr[pl.ds(r0, CHUNK), :] for r0 in r0s]
        g = [g_scr[pl.ds(r0, CHUNK), :] for r0 in r0s]
        kb = [kc * bc for kc, bc in zip(k, beta)]
        aa = [_dot_nt(jnp.concatenate([kbc.astype(BF16), qc.astype(BF16)], axis=0), bd(kc.astype(BF16)))
              for kbc, qc, kc in zip(kb, q, k)]
        g_row = [_dot01_l(ones_ref[...], gc * eye) for gc in g]
        yield
        decay = [jnp.exp(jnp.minimum(gc - grc, 0.0)) for gc, grc in zip(g, g_row)]
        m = [ac[0:CHUNK] * dc * strict_ref[...] for ac, dc in zip(aa, decay)]
        for r0, ac, dc in zip(r0s, aa, decay):
            aqk_scr[pl.ds(r0, CHUNK), :] = (ac[CHUNK:2 * CHUNK] * dc * causal_ref[...]).astype(BF16)
        pm = [eye - mc for mc in m]
        qm = [_dot(mc.astype(BF16), bd(mc.astype(BF16))) for mc in m]
        yield
        for it in range(5):
            q16 = [qc.astype(BF16) for qc in qm]
            if it < 4:
                r = [_dot(jnp.concatenate([pc.astype(BF16), qc], axis=0), bd(qc)) for pc, qc in zip(pm, q16)]
                pm = [pc + rc[0:CHUNK] for pc, rc in zip(pm, r)]
                qm = [rc[CHUNK:2 * CHUNK] for rc in r]
            else:
                pm = [pc + _dot(pc.astype(BF16), bd(qc)) for pc, qc in zip(pm, q16)]
            yield
        t16 = [pc.astype(BF16) for pc in pm]
        eg = [jnp.exp(gc) for gc in g]
        for c, (n, r0) in enumerate(zip(ns, r0s)):
            v = v_scr[pl.ds(r0, CHUNK), :]
            g_last = g_scr[pl.ds(r0 + CHUNK - 1, 1), :]
            wq_scr[n, 0:CHUNK, :] = _dot(t16[c], bd((kb[c] * eg[c]).astype(BF16))).astype(BF16)
            wq_scr[n, CHUNK:2 * CHUNK, :] = (q[c] * eg[c]).astype(BF16)
            u_scr[pl.ds(r0, CHUNK), :] = _dot(t16[c], bd((v * beta[c]).astype(BF16)))
            kout_scr[pl.ds(r0, CHUNK), :] = (k[c] * jnp.exp(g_last - g[c])).astype(BF16)

    def scan_stages(i):
        st = s_scr[...]
        for c in range(WY_GROUP):
            n = i * WY_GROUP + c
            r0 = pl.multiple_of(n * CHUNK, CHUNK)
            wq = _dot(wq_scr[n], st.astype(BF16))
            yield
            v16 = (u_scr[pl.ds(r0, CHUNK), :] - wq[0:CHUNK]).astype(BF16)
            a_last = jnp.exp(g_scr[pl.ds(r0 + CHUNK - 1, 1), :])
            st = st * a_last + _dot_tn(kout_scr[pl.ds(r0, CHUNK), :], v16) * bdf
            o_scr[pl.ds(r0, CHUNK), :] = wq[CHUNK:2 * CHUNK] + _dot(aqk_scr[pl.ds(r0, CHUNK), :], bd(v16))
            yield
        s_scr[...] = st

    n_groups = s_len // (WY_GROUP * CHUNK)
    s_scr[...] = jnp.zeros((GROUP_W, GROUP_W), F32)
    _interleave(wy_stages(0))

    def group_step(i, _):
        _interleave(wy_stages(i), scan_stages(i - 1))
        return 0

    lax.fori_loop(1, n_groups, group_step, 0)
    _interleave(scan_stages(n_groups - 1))

    def out_blk(i, _):
        r0 = pl.multiple_of(i * rb, rb)
        o = o_scr[pl.ds(r0, rb), :]
        on = o * lax.rsqrt(_seg_meansq(o, bd16, HEAD_DIM) + EPS) * og_ref[...]
        o_ref[pl.ds(r0, rb), :] = (on * _silu(z_ref[pl.ds(r0, rb), :])).astype(BF16)
        return 0

    lax.fori_loop(0, nblk, out_blk, 0)


def _gdn(gdn_qkv, gdn_z, gates, cw, alog_row, dt_row, og_row, consts, b, s):
    rb = min(256, s)
    eye, strict, causal = consts["chunk_masks"]
    kern = functools.partial(_gdn_kernel, rb=rb)
    row = lambda w: pl.BlockSpec((s, w), lambda i: (i, 0))
    return pl.pallas_call(
        kern,
        grid=(b,),
        in_specs=[row(3 * GROUP_W), row(GROUP_W), row(GATE_W),
                  _full_spec((8, 3 * GROUP_W)), _full_spec((1, GATE_W)), _full_spec((1, GATE_W)),
                  _full_spec((1, GROUP_W)), _full_spec((GROUP_W, GROUP_W)),
                  _full_spec((GATE_W, GROUP_W)), _full_spec((GATE_W, GROUP_W)),
                  _full_spec((CHUNK, GROUP_W)), _full_spec((CHUNK, GROUP_W)), _full_spec((CHUNK, GROUP_W)),
                  _full_spec((CHUNK, CHUNK)), _full_spec((rb, rb))],
        out_specs=row(GROUP_W),
        out_shape=jax.ShapeDtypeStruct((b * s, GROUP_W), BF16),
        scratch_shapes=[pltpu.VMEM((s + 8, 3 * GROUP_W), F32)]
                       + [pltpu.VMEM((s, GROUP_W), F32) for _ in range(7)]
                       + [pltpu.VMEM((GROUP_W, GROUP_W), F32),
                          pltpu.VMEM((s // CHUNK, 2 * CHUNK, GROUP_W), BF16),
                          pltpu.VMEM((s, GROUP_W), BF16), pltpu.VMEM((s, GROUP_W), BF16)],
        compiler_params=_cparams(1),
        name="gdn",
    )(gdn_qkv, gdn_z, gates, cw, alog_row, dt_row, og_row, consts["bd256"], consts["expand_b"], consts["expand_g"],
      eye, strict, causal, consts["ones64"],
      jnp.asarray(_tril_ones(rb) * _block_mask(rb, CHUNK, rb, CHUNK), BF16))


def _la_kernel(*refs, variant, rb):
    if variant == "hgrn2":
        (x_ref, la_ref, l1_ref, oml_ref, og_ref, bdv_ref, bdk_ref, eye_ref, lvl_ref, ltri_ref,
         o_ref, q_scr, k_scr, a_scr, v_scr, gate_scr, gc_scr, o_scr, st_scr) = refs
        kw, dk = GROUP_W, HEAD_DIM
    else:
        (x_ref, gates_ref, wgk_ref, bgk_ref, og_ref, bdv_ref, bdk_ref, eye_ref, lvl_ref, ltri_ref,
         o_ref, q_scr, k_scr, a_scr, v_scr, gate_scr, gc_scr, o_scr, st_scr) = refs
        kw, dk = GLA_KW, GLA_DK
    s_len = x_ref.shape[0]
    nblk = s_len // rb
    bdv16 = bdv_ref[...]
    bdk16 = bdk_ref[...]
    bdkf = bdk16.astype(F32)

    def prep_blk(i, _):
        r0 = pl.multiple_of(i * rb, rb)
        x = x_ref[pl.ds(r0, rb), :]
        if variant == "hgrn2":
            f_logit = x[:, GROUP_W:2 * GROUP_W]
            q_scr[pl.ds(r0, rb), :] = _silu(x[:, 0:GROUP_W]) * (dk ** -0.5)
            a = la_ref[...]
            bterm = l1_ref[...] + _log_sigmoid(f_logit)
            amax = jnp.maximum(a, bterm)
            a_scr[pl.ds(r0, rb), :] = amax + jnp.log1p(jnp.exp(-jnp.abs(a - bterm)))
            k_scr[pl.ds(r0, rb), :] = oml_ref[...] * jax.nn.sigmoid(-f_logit)
            v_scr[pl.ds(r0, rb), :] = x[:, 2 * GROUP_W:3 * GROUP_W]
            gate_scr[pl.ds(r0, rb), :] = x[:, 3 * GROUP_W:4 * GROUP_W]
        else:
            q_scr[pl.ds(r0, rb), :] = x[:, 0:kw] * (dk ** -0.5)
            k_scr[pl.ds(r0, rb), :] = x[:, kw:2 * kw]
            v_scr[pl.ds(r0, rb), :] = x[:, 2 * kw:2 * kw + GROUP_W]
            gate_scr[pl.ds(r0, rb), :] = x[:, 2 * kw + GROUP_W:2 * kw + 2 * GROUP_W]
            lr = _dot(gates_ref[pl.ds(r0, rb), :].astype(BF16), wgk_ref[...]) + bgk_ref[...]
            a_scr[pl.ds(r0, rb), :] = _log_sigmoid(lr) * (1.0 / GLA_NORM)
        return 0

    lax.fori_loop(0, nblk, prep_blk, 0)

    st_scr[...] = jnp.zeros((GROUP_W, kw), F32)
    sub = lax.broadcasted_iota(jnp.int32, (8, kw), 0)

    def level_ref(m, slot):
        def brow(r):
            return jnp.broadcast_to(gc_scr[slot, r:r + 1, :], (8, kw))
        pieces = []
        for a in range(CHUNK // 8):
            if 2 * m >= 8:
                pieces.append(brow((8 * a) // (2 * m) * (2 * m) + m - 1))
            elif m == 2:
                pieces.append(jnp.where(sub < 4, brow(8 * a + 1), brow(8 * a + 5)))
            else:
                p = jnp.where(sub < 2, brow(8 * a), brow(8 * a + 2))
                p = jnp.where(sub < 4, p, jnp.where(sub < 6, brow(8 * a + 4), brow(8 * a + 6)))
                pieces.append(p)
        return jnp.concatenate(pieces, axis=0)

    def chunk_group(i, _):
        slots = range(WY_GROUP)
        r0s = [pl.multiple_of((i * WY_GROUP + c) * CHUNK, CHUNK) for c in slots]
        q = [q_scr[pl.ds(r0, CHUNK), :] for r0 in r0s]
        k = [k_scr[pl.ds(r0, CHUNK), :] for r0 in r0s]
        v16 = [v_scr[pl.ds(r0, CHUNK), :].astype(BF16) for r0 in r0s]
        g = [_dot01_l(ltri_ref[...], a_scr[pl.ds(r0, CHUNK), :]) for r0 in r0s]
        for c in slots:
            gc_scr[c] = g[c]
        a_in = [_dot_nt(q[c].astype(BF16), _tile_rows(k[c].astype(BF16), N_HEADS) * bdk16) * eye_ref[...]
                for c in slots]
        for li, m in enumerate(LEVELS):
            for c in slots:
                e = jnp.exp(-jnp.abs(g[c] - level_ref(m, c)))
                qe = (q[c] * e).astype(BF16)
                ke = (k[c] * e).astype(BF16)
                a_in[c] = a_in[c] + _dot_nt(qe, _tile_rows(ke, N_HEADS) * bdk16) * lvl_ref[li]
        o = [_dot(a_in[c].astype(BF16), _tile_rows(v16[c], N_HEADS) * bdv16) for c in slots]
        ds = [_dot_tn(v16[c], (k[c] * jnp.exp(g[c][CHUNK - 1:CHUNK, :] - g[c])).astype(BF16)) * bdkf for c in slots]
        q_in = [(q[c] * jnp.exp(g[c])).astype(BF16) for c in slots]
        st = st_scr[...]
        for c in slots:
            o_scr[pl.ds(r0s[c], CHUNK), :] = o[c] + _dot_nt(q_in[c], st.astype(BF16))
            st = st * jnp.exp(g[c][CHUNK - 1:CHUNK, :]) + ds[c]
        st_scr[...] = st
        return 0

    lax.fori_loop(0, s_len // (WY_GROUP * CHUNK), chunk_group, 0)

    def out_blk(i, _):
        r0 = pl.multiple_of(i * rb, rb)
        o = o_scr[pl.ds(r0, rb), :]
        on = o * lax.rsqrt(_seg_meansq(o, bdv16, HEAD_DIM) + EPS) * og_ref[...]
        o_ref[pl.ds(r0, rb), :] = (on * _silu(gate_scr[pl.ds(r0, rb), :])).astype(BF16)
        return 0

    lax.fori_loop(0, nblk, out_blk, 0)


def _la(variant, x, extra, og_row, consts, b, s):
    rb = min(256, s)
    kw = GROUP_W if variant == "hgrn2" else GLA_KW
    xw = 4 * GROUP_W if variant == "hgrn2" else 3 * GROUP_W
    kern = functools.partial(_la_kernel, variant=variant, rb=rb)
    row = lambda w: pl.BlockSpec((s, w), lambda i: (i, 0))
    if variant == "hgrn2":
        in_specs = [row(xw)] + [_full_spec((1, GROUP_W))] * 3
        bdk = consts["bd256"]
    else:
        in_specs = [row(xw), row(GATE_W), _full_spec((GATE_W, GLA_KW)), _full_spec((1, GLA_KW))]
        bdk = consts["bdk_gla"]
    in_specs += [_full_spec((1, GROUP_W)), _full_spec((GROUP_W, GROUP_W)), _full_spec((GROUP_W, kw)),
                 _full_spec((CHUNK, GROUP_W)), _full_spec((len(LEVELS), CHUNK, GROUP_W)), _full_spec((CHUNK, CHUNK))]
    return pl.pallas_call(
        kern,
        grid=(b,),
        in_specs=in_specs,
        out_specs=row(GROUP_W),
        out_shape=jax.ShapeDtypeStruct((b * s, GROUP_W), BF16),
        scratch_shapes=[pltpu.VMEM((s, kw), F32) for _ in range(3)]
                       + [pltpu.VMEM((s, GROUP_W), F32) for _ in range(2)]
                       + [pltpu.VMEM((WY_GROUP, CHUNK, kw), F32), pltpu.VMEM((s, GROUP_W), F32),
                          pltpu.VMEM((GROUP_W, kw), F32)],
        compiler_params=_cparams(1),
        name="la_" + variant,
    )(x, *extra, og_row, consts["bd256"], bdk, consts["chunk_masks"][0], consts["level_masks"], consts["ltri64"])


FFN_TAIL = 16


def _ffn_kernel(x_ref, oa_ref, ob_ref, oc_ref, od_ref, wo_ref, g2_ref, wup_ref, ctap_ref, wd_ref,
                out_ref, hext, u_scr, act_scr, *, tm):
    it = pl.program_id(1)
    x1 = x_ref[...]
    for kk, o_ref in enumerate((oa_ref, ob_ref, oc_ref, od_ref)):
        x1 = x1 + _dot(o_ref[...], wo_ref[kk])
    out_ref[...] = x1
    ms = jnp.mean(x1 * x1, axis=-1, keepdims=True)
    h2 = (x1 * lax.rsqrt(ms + EPS) * g2_ref[...]).astype(BF16)

    @pl.when(it == 0)
    def _():
        hext[0:FFN_TAIL, :] = jnp.zeros((FFN_TAIL, D_MODEL), BF16)

    hext[FFN_TAIL:, :] = h2

    def conv(col0, slot):
        u_scr[slot] = _dot(hext[...], wup_ref[:, col0:col0 + FF_CHUNK])
        c = ctap_ref[:, col0:col0 + FF_CHUNK]
        y = c[0:1, :] * u_scr[slot, FFN_TAIL - 2:FFN_TAIL - 2 + tm, :]
        y = y + c[1:2, :] * u_scr[slot, FFN_TAIL - 1:FFN_TAIL - 1 + tm, :]
        y = y + c[2:3, :] * u_scr[slot, FFN_TAIL:FFN_TAIL + tm, :]
        return y + c[3:4, :]

    for j in range(D_FF // FF_CHUNK):
        slot = 2 * (j % 2)
        gate = conv(j * FF_CHUNK, slot)
        up = conv(D_FF + j * FF_CHUNK, slot + 1)
        act_scr[:, j * FF_CHUNK:(j + 1) * FF_CHUNK] = (_silu(gate) * up).astype(BF16)
    out_ref[...] += _dot(act_scr[...], wd_ref[...])
    hext[0:FFN_TAIL, :] = hext[tm:tm + FFN_TAIL, :]


def _ffn(x2d, outs, wo, g2, wup, ctap, wd, b, s):
    tm = min(ROW_TILE, s)
    nt = s // tm
    kern = functools.partial(_ffn_kernel, tm=tm)
    row = lambda w: pl.BlockSpec((tm, w), lambda i, j: (i * nt + j, 0))
    return pl.pallas_call(
        kern,
        grid=(b, nt),
        in_specs=[row(D_MODEL)] + [row(GROUP_W)] * 4
                 + [_full_spec((4, GROUP_W, D_MODEL)), _full_spec((1, D_MODEL)),
                    _full_spec((D_MODEL, 2 * D_FF)), _full_spec((8, 2 * D_FF)), _full_spec((D_FF, D_MODEL))],
        out_specs=row(D_MODEL),
        out_shape=jax.ShapeDtypeStruct(x2d.shape, F32),
        scratch_shapes=[pltpu.VMEM((tm + FFN_TAIL, D_MODEL), BF16),
                        pltpu.VMEM((4, tm + FFN_TAIL, FF_CHUNK), F32),
                        pltpu.VMEM((tm, D_FF), BF16)],
        compiler_params=_cparams(2),
        name="outproj_ffn",
    )(x2d, *outs, wo, g2, wup, ctap, wd)


def _constants():
    eye, strict, causal = _chunk_masks()
    return {
        "bd256": jnp.asarray(_block_mask(GROUP_W, HEAD_DIM, GROUP_W, HEAD_DIM), BF16),
        "bdk_gla": jnp.asarray(_block_mask(GROUP_W, HEAD_DIM, GLA_KW, GLA_DK), BF16),
        "expand_b": jnp.asarray(_expand_mat(GATE_GDN_B), BF16),
        "expand_g": jnp.asarray(_expand_mat(GATE_GDN_A), BF16),
        "chunk_masks": (jnp.asarray(eye), jnp.asarray(strict), jnp.asarray(causal)),
        "level_masks": jnp.asarray(_level_masks()),
        "ones64": jnp.ones((CHUNK, CHUNK), BF16),
        "ltri64": jnp.asarray(_tril_ones(CHUNK), BF16),
        "fox_routes": tuple(jnp.asarray(a, BF16) if a.ndim == 2 else jnp.asarray(a) for a in _fox_routes()),
    }


def _w_in_moves():
    sizes = (3 * GROUP_W, N_HEADS, 3 * GROUP_W, N_HEADS, N_HEADS, GROUP_W,
             GROUP_W, GROUP_W, GROUP_W, GROUP_W, 2 * GLA_KW, GROUP_W, GLA_RANK, GROUP_W)
    names = ("fox_qkv", "fox_f", "gdn_qkv", "gdn_b", "gdn_a", "gdn_z", "hg_q", "hg_f", "hg_i", "hg_g",
             "gla_qk", "gla_v", "gla_gk", "gla_g")
    src = dict(zip(names, np.concatenate([[0], np.cumsum(sizes)[:-1]])))
    wid = dict(zip(names, sizes))
    order = ("fox_qkv", "gdn_qkv", "gdn_z", "hg_q", "hg_f", "hg_i", "hg_g", "gla_qk", "gla_v", "gla_g",
             "fox_f", "gdn_b", "gdn_a", "gla_gk")
    moves, dst = [], 0
    for nm in order:
        moves.append((int(src[nm]), dst, wid[nm]))
        dst += wid[nm]
    return moves, dst


def _regroup_kernel(w_ref, o_ref):
    moves, used = _w_in_moves()
    x = w_ref[0]
    for src, dst, wd in moves:
        o_ref[0, :, dst:dst + wd] = x[:, src:src + wd].astype(BF16)
    o_ref[0, :, used:IN_COLS_PAD] = jnp.zeros((x.shape[0], IN_COLS_PAD - used), BF16)


def _regroup_w_in(w_in):
    depth, d, cols = w_in.shape
    tr = 256
    return pl.pallas_call(
        _regroup_kernel,
        grid=(depth, d // tr),
        in_specs=[pl.BlockSpec((1, tr, cols), lambda l, i: (l, i, 0))],
        out_specs=pl.BlockSpec((1, tr, IN_COLS_PAD), lambda l, i: (l, i, 0)),
        out_shape=jax.ShapeDtypeStruct((depth, d, IN_COLS_PAD), BF16),
        compiler_params=_cparams(2),
        name="regroup_w_in",
    )(w_in)


def _lane_row(vals, lane0, width):
    return jnp.zeros((1, width), F32).at[0, lane0:lane0 + vals.shape[0]].set(vals.astype(F32))


def _tile_heads(g):
    return jnp.tile(g.astype(F32), N_HEADS)[None, :]


def kernel(x, norm1_g, w_in, fox_qn_g, fox_kn_g, fox_b_f, fox_on_g, gdn_conv_w, gdn_a_log, gdn_dt_bias, gdn_on_g,
           hg_lb, hg_on_g, gla_w_gk, gla_b_gk, gla_on_g, w_out, norm2_g, w_up, ffn_conv_w, ffn_conv_b, w_down):
    b, s, d = x.shape
    depth = w_in.shape[0]
    consts = _constants()

    cs = jnp.cumsum(jax.nn.softmax(hg_lb.astype(F32), axis=0), axis=0)
    lower = cs - cs[0:1]

    w_in_all = _regroup_w_in(w_in)
    x2d = x.reshape(b * s, d)
    for l in range(depth):
        fox_qkv, gdn_qkv, gdn_z, hg, gla, gates = _inproj(x2d, norm1_g[l][None, :], w_in_all, l)

        qa, ka, va = _fox_prep(fox_qkv, gates, _lane_row(fox_b_f[l], GATE_FOX_F, GATE_W),
                               _tile_heads(fox_qn_g[l]), _tile_heads(fox_kn_g[l]), consts, b, s)
        o_a = _fox_attn(qa, ka, va, jnp.tile(fox_on_g[l].astype(F32), 2)[None, :], b, s)

        cw = jnp.zeros((8, 3 * GROUP_W), F32).at[0:GDN_CONV].set(gdn_conv_w[l])
        o_b = _gdn(gdn_qkv, gdn_z, gates, cw, _lane_row(gdn_a_log[l], GATE_GDN_A, GATE_W),
                   _lane_row(gdn_dt_bias[l], GATE_GDN_A, GATE_W), _tile_heads(gdn_on_g[l]), consts, b, s)

        lb = lower[l][None, :]
        o_c = _la("hgrn2", hg, (jnp.log(lb), jnp.log1p(-lb), 1.0 - lb), _tile_heads(hg_on_g[l]), consts, b, s)

        wgk = jnp.zeros((GATE_W, GLA_KW), F32).at[GATE_GLA_GK:GATE_GLA_GK + GLA_RANK].set(gla_w_gk[l]).astype(BF16)
        o_d = _la("gla", gla, (gates, wgk, gla_b_gk[l][None, :].astype(F32)), _tile_heads(gla_on_g[l]), consts, b, s)

        wo = w_out[l].reshape(4, GROUP_W, d).astype(BF16)
        taps = jnp.concatenate([ffn_conv_w[l], ffn_conv_b[l][None, :],
                                jnp.zeros((8 - FFN_CONV - 1, 2 * D_FF), F32)], axis=0)
        x2d = _ffn(x2d, (o_a, o_b, o_c, o_d), wo, norm2_g[l][None, :], w_up[l].astype(BF16), taps,
                   w_down[l].astype(BF16), b, s)
    return x2d.reshape(b, s, d)
```

```python
import functools

import numpy as np
import jax
import jax.numpy as jnp
from jax import lax
from jax.experimental import pallas as pl
from jax.experimental.pallas import tpu as pltpu

F32 = jnp.float32
BF16 = jnp.bfloat16

D_MODEL = 1024
N_HEADS = 4
HEAD_DIM = 64
GROUP_W = N_HEADS * HEAD_DIM
GLA_DK = 32
GLA_KW = N_HEADS * GLA_DK
GLA_RANK = 16
GLA_NORM = 16.0
GDN_CONV = 4
D_FF = 2816
FFN_CONV = 3
EPS = 1e-6
CHUNK = 64
WY_GROUP = 4

LANES = 128
GATE_W = LANES
GATE_FOX_F, GATE_GDN_B, GATE_GDN_A, GATE_GLA_GK = 0, 4, 8, 12

IN_SEGS = (3 * GROUP_W, 3 * GROUP_W, GROUP_W, 4 * GROUP_W, 3 * GROUP_W, GATE_W)
IN_COLS_PAD = sum(IN_SEGS)

ROW_TILE = 512
FF_CHUNK = 256
FOX_BLOCK = 512
NEG_BIG = -1e30

VMEM_LIMIT = 56 * 1024 * 1024


def _dot(a, b):
    return jnp.dot(a, b, preferred_element_type=F32)


def _dot_nt(a, b):
    return lax.dot_general(a, b, (((1,), (1,)), ((), ())), preferred_element_type=F32)


def _dot_tn(a, b):
    return lax.dot_general(a, b, (((0,), (0,)), ((), ())), preferred_element_type=F32)


def _split3(x):
    hi = x.astype(BF16)
    r1 = x - hi.astype(F32)
    mid = r1.astype(BF16)
    lo = (r1 - mid.astype(F32)).astype(BF16)
    return hi, mid, lo


def _dot01_l(m01, x):
    hi, mid, lo = _split3(x)
    return _dot(m01, hi) + _dot(m01, mid) + _dot(m01, lo)


def _dot01_r(x, m01):
    hi, mid, lo = _split3(x)
    return _dot(hi, m01) + _dot(mid, m01) + _dot(lo, m01)


def _seg_meansq(x, bd16, width):
    sq = x * x
    hi = sq.astype(BF16)
    lo = (sq - hi.astype(F32)).astype(BF16)
    return (_dot(hi, bd16) + _dot(lo, bd16)) * (1.0 / width)


def _log_sigmoid(x):
    return jnp.minimum(x, 0.0) - jnp.log1p(jnp.exp(-jnp.abs(x)))


def _softplus(x):
    return jnp.maximum(x, 0.0) + jnp.log1p(jnp.exp(-jnp.abs(x)))


def _silu(x):
    return x * jax.nn.sigmoid(x)


def _tile_rows(x, n):
    return jnp.concatenate([x] * n, axis=0)


def _interleave(*gens):
    live = list(gens)
    while live:
        for gen in list(live):
            try:
                next(gen)
            except StopIteration:
                live.remove(gen)


def _software_pipeline(phases, n_steps):
    def step(t):
        _interleave(*[ph(t - d) for d, ph in enumerate(phases) if isinstance(t, jax.Array) or 0 <= t - d < n_steps])

    depth = len(phases) - 1
    for t in range(n_steps + depth):
        if depth <= t < n_steps:
            if t == depth:
                lax.fori_loop(depth, n_steps, lambda tt, c: (step(tt), c)[1], 0)
        else:
            step(t)


def _full_spec(shape):
    nd = len(shape)
    return pl.BlockSpec(shape, lambda *_: (0,) * nd, pipeline_mode=pl.Buffered(1))


def _cparams(n_axes):
    return pltpu.CompilerParams(dimension_semantics=("arbitrary",) * n_axes, vmem_limit_bytes=VMEM_LIMIT)


def _block_mask(rows, row_blk, cols, col_blk):
    r = np.arange(rows)[:, None] // row_blk
    c = np.arange(cols)[None, :] // col_blk
    return (r == c).astype(np.float32)


def _tril_ones(n):
    return np.tril(np.ones((n, n), np.float32))


def _expand_mat(lane0):
    m = np.zeros((GATE_W, GROUP_W), np.float32)
    for h in range(N_HEADS):
        m[lane0 + h, h * HEAD_DIM:(h + 1) * HEAD_DIM] = 1.0
    return m


def _chunk_masks():
    r = np.arange(CHUNK)[:, None]
    s = np.arange(GROUP_W)[None, :] % CHUNK
    return ((r == s).astype(np.float32), (r > s).astype(np.float32), (r >= s).astype(np.float32))


LEVELS = (1, 2, 4, 8, 16, 32)


def _level_masks():
    r = np.arange(CHUNK)[:, None]
    s = np.arange(GROUP_W)[None, :] % CHUNK
    out = []
    for m in LEVELS:
        same = (r // (2 * m)) == (s // (2 * m))
        out.append((same & ((r % (2 * m)) >= m) & ((s % (2 * m)) < m)).astype(np.float32))
    return np.stack(out)


def _fox_routes():
    rq = np.zeros((3 * GATE_W, N_HEADS * LANES), np.float32)
    rk = np.zeros((3 * GATE_W, N_HEADS * LANES), np.float32)
    oq = np.zeros((N_HEADS, 1, LANES), np.float32)
    ok = np.zeros((N_HEADS, 1, LANES), np.float32)
    for h in range(N_HEADS):
        base = (1 - h % 2) * HEAD_DIM
        for j in range(3):
            rq[j * GATE_W + GATE_FOX_F + h, h * LANES + base + j] = 1.0
            rk[j * GATE_W + GATE_FOX_F + h, h * LANES + base + 3 + j] = -1.0
            oq[h, 0, base + 3 + j] = 1.0
            ok[h, 0, base + j] = 1.0
    return rq, rk, oq, ok


def _inproj_kernel(x_ref, g_ref, w_ref, *out_refs):
    x = x_ref[...]
    ms = jnp.mean(x * x, axis=-1, keepdims=True)
    h = (x * lax.rsqrt(ms + EPS) * g_ref[...]).astype(BF16)
    off = 0
    for ref, width in zip(out_refs, IN_SEGS):
        step = min(width, 256)
        for c in range(0, width, step):
            ref[:, c:c + step] = _dot(h, w_ref[:, off + c:off + c + step])
        off += width


def _inproj(x2d, g, w_all, layer):
    t = x2d.shape[0]
    tm = min(ROW_TILE, t)
    return pl.pallas_call(
        _inproj_kernel,
        grid=(t // tm,),
        in_specs=[pl.BlockSpec((tm, D_MODEL), lambda i: (i, 0)),
                  _full_spec((1, D_MODEL)),
                  pl.BlockSpec((None, D_MODEL, IN_COLS_PAD), lambda i: (layer, 0, 0), pipeline_mode=pl.Buffered(1))],
        out_specs=[pl.BlockSpec((tm, wd), lambda i: (i, 0)) for wd in IN_SEGS],
        out_shape=[jax.ShapeDtypeStruct((t, wd), F32) for wd in IN_SEGS],
        compiler_params=_cparams(1),
        name="inproj",
    )(x2d, g, w_all)


def _fox_prep_kernel(qkv_ref, gates_ref, bf_ref, qg_ref, kg_ref, bd_ref, ltri_ref, rq_ref, rk_ref, oq_ref, ok_ref,
                     qa_ref, ka_ref, va_ref, *, rb):
    s_len = qkv_ref.shape[0]
    lane = lax.broadcasted_iota(jnp.int32, (rb, LANES), 1)

    def blk(i, carry):
        r0 = pl.multiple_of(i * rb, rb)
        logf = _log_sigmoid(gates_ref[pl.ds(r0, rb), :] + bf_ref[...])
        c = _dot01_l(ltri_ref[...], logf) + carry
        c3 = jnp.concatenate(_split3(c), axis=1)
        qkv = qkv_ref[pl.ds(r0, rb), :]
        q = qkv[:, 0:GROUP_W]
        k = qkv[:, GROUP_W:2 * GROUP_W]
        v = qkv[:, 2 * GROUP_W:3 * GROUP_W]
        bd = bd_ref[...]
        qn = q * lax.rsqrt(_seg_meansq(q, bd, HEAD_DIM) + EPS) * qg_ref[...] * (HEAD_DIM ** -0.5)
        kn = k * lax.rsqrt(_seg_meansq(k, bd, HEAD_DIM) + EPS) * kg_ref[...]
        q_route = _dot(c3, rq_ref[...])
        k_route = _dot(c3, rk_ref[...])
        for h in range(N_HEADS):
            p = h // 2
            own = (lane // HEAD_DIM) == (h % 2)
            qa = jnp.where(own, qn[:, p * LANES:(p + 1) * LANES], q_route[:, h * LANES:(h + 1) * LANES] + oq_ref[h])
            ka = jnp.where(own, kn[:, p * LANES:(p + 1) * LANES], k_route[:, h * LANES:(h + 1) * LANES] + ok_ref[h])
            va = jnp.where(own, v[:, p * LANES:(p + 1) * LANES], jnp.where(lane == (1 - h % 2) * HEAD_DIM, 1.0, 0.0))
            qa_ref[0, h, pl.ds(r0, rb), :] = qa.astype(BF16)
            ka_ref[0, h, pl.ds(r0, rb), :] = ka.astype(BF16)
            va_ref[0, h, pl.ds(r0, rb), :] = va.astype(BF16)
        return c[rb - 1:rb, :]

    lax.fori_loop(0, s_len // rb, blk, jnp.zeros((1, GATE_W), F32))


def _fox_prep(fox_qkv, gates, bf_row, qg_row, kg_row, consts, b, s):
    rb = min(256, s)
    rq, rk, oq, ok = consts["fox_routes"]
    ltri = jnp.asarray(_tril_ones(rb), BF16)
    kern = functools.partial(_fox_prep_kernel, rb=rb)
    return pl.pallas_call(
        kern,
        grid=(b,),
        in_specs=[pl.BlockSpec((s, 3 * GROUP_W), lambda i: (i, 0)),
                  pl.BlockSpec((s, GATE_W), lambda i: (i, 0)),
                  _full_spec((1, GATE_W)), _full_spec((1, GROUP_W)), _full_spec((1, GROUP_W)),
                  _full_spec((GROUP_W, GROUP_W)), _full_spec((rb, rb)),
                  _full_spec(rq.shape), _full_spec(rk.shape), _full_spec(oq.shape), _full_spec(ok.shape)],
        out_specs=[pl.BlockSpec((1, N_HEADS, s, LANES), lambda i: (i, 0, 0, 0)),
                   pl.BlockSpec((1, N_HEADS, s, LANES), lambda i: (i, 0, 0, 0)),
                   pl.BlockSpec((1, N_HEADS, s, LANES), lambda i: (i, 0, 0, 0))],
        out_shape=[jax.ShapeDtypeStruct((b, N_HEADS, s, LANES), BF16)] * 3,
        compiler_params=_cparams(1),
        name="fox_prep",
    )(fox_qkv, gates, bf_row, qg_row, kg_row, consts["bd256"], ltri, rq, rk, oq, ok)


def _fox_attn_kernel(q_ref, k_ref, v_ref, og_ref, o_ref, s_scr, m_scr, acc_scr, *, blk, nq):
    lane = lax.broadcasted_iota(jnp.int32, (blk, LANES), 1)
    row = lax.broadcasted_iota(jnp.int32, (blk, blk), 0)
    col = lax.broadcasted_iota(jnp.int32, (blk, blk), 1)

    def fold(s):
        m = s[:, 0:LANES]
        for c in range(1, blk // LANES):
            m = jnp.maximum(m, s[:, c * LANES:(c + 1) * LANES])
        return m

    def logit_stages(i):
        slot = i % 2
        for j in range(i + 1):
            for hh in range(2):
                s = _dot_nt(q_ref[0, hh, i * blk:(i + 1) * blk, :], k_ref[0, hh, j * blk:(j + 1) * blk, :])
                if j == i:
                    s = jnp.where(col <= row, s, NEG_BIG)
                s_scr[slot, hh, j] = s
                m_scr[slot, hh] = fold(s) if j == 0 else jnp.maximum(m_scr[slot, hh], fold(s))
            yield

    def value_stages(i):
        slot = i % 2
        row_max = [jnp.max(m_scr[slot, hh], axis=-1, keepdims=True) for hh in range(2)]
        for j in range(i + 1):
            for hh in range(2):
                p = jnp.exp(s_scr[slot, hh, j] - row_max[hh])
                pv = _dot(p.astype(BF16), v_ref[0, hh, j * blk:(j + 1) * blk, :])
                acc_scr[slot, hh] = pv if j == 0 else acc_scr[slot, hh] + pv
            yield
        outs = []
        for hh in range(2):
            acc = acc_scr[slot, hh]
            l = jnp.sum(jnp.where(lane == (1 - hh) * HEAD_DIM, acc, 0.0), axis=-1, keepdims=True)
            o = acc / l
            own = (lane // HEAD_DIM) == hh
            ms = jnp.sum(jnp.where(own, o * o, 0.0), axis=-1, keepdims=True) * (1.0 / HEAD_DIM)
            outs.append(o * lax.rsqrt(ms + EPS) * og_ref[...])
        o_ref[i * blk:(i + 1) * blk, :] = jnp.where((lane // HEAD_DIM) == 0, outs[0], outs[1]).astype(BF16)

    _interleave(logit_stages(0))
    for i in range(nq):
        if i + 1 < nq:
            _interleave(value_stages(i), logit_stages(i + 1))
        else:
            _interleave(value_stages(i))


def _fox_attn(qa, ka, va, og_row, b, s):
    blk = min(FOX_BLOCK, s)
    nq = s // blk
    kern = functools.partial(_fox_attn_kernel, blk=blk, nq=nq)
    head_pair = pl.BlockSpec((1, 2, s, LANES), lambda i, p: (i, p, 0, 0))
    return pl.pallas_call(
        kern,
        grid=(b, 2),
        in_specs=[head_pair, head_pair, head_pair, _full_spec((1, LANES))],
        out_specs=pl.BlockSpec((s, LANES), lambda i, p: (i, p)),
        out_shape=jax.ShapeDtypeStruct((b * s, GROUP_W), BF16),
        scratch_shapes=[pltpu.VMEM((2, 2, nq, blk, blk), F32),
                        pltpu.VMEM((2, 2, blk, LANES), F32),
                        pltpu.VMEM((2, 2, blk, LANES), F32)],
        compiler_params=_cparams(2),
        name="fox_attn",
    )(qa, ka, va, og_row)


def _gdn_kernel(qkv_ref, z_ref, gates_ref, cw_ref, alog_ref, dt_ref, og_ref, bd_ref, eb_ref, eg_ref,
                eye_ref, strict_ref, causal_ref, ones_ref, ctri_ref,
                o_ref, xpad, q_scr, k_scr, v_scr, beta_scr, g_scr, o_scr, u_scr, s_scr,
                wq_scr, aqk_scr, kout_scr, *, rb):
    s_len = qkv_ref.shape[0]
    nblk = s_len // rb
    bd16 = bd_ref[...]
    bdf = bd16.astype(F32)

    xpad[0:8, :] = jnp.zeros((8, 3 * GROUP_W), F32)
    xpad[8:, :] = qkv_ref[...]

    def prep_stages(i):
        r0 = pl.multiple_of(i * rb, rb)

        gt = gates_ref[pl.ds(r0, rb), :]
        beta_scr[pl.ds(r0, rb), :] = _dot01_r(jax.nn.sigmoid(gt), eb_ref[...])
        g_s = -jnp.exp(alog_ref[...]) * _softplus(gt + dt_ref[...])
        g_rep = [_dot(t, eg_ref[...]).astype(BF16) for t in _split3(g_s)]

        def conv_silu(c0):
            xx = xpad[pl.ds(r0, rb + 8), c0:c0 + LANES]
            y = cw_ref[0:1, c0:c0 + LANES] * xx[5:5 + rb]
            for j in range(1, GDN_CONV):
                y = y + cw_ref[j:j + 1, c0:c0 + LANES] * xx[5 + j:5 + j + rb]
            return _silu(y)

        yield
        q_lo = conv_silu(0)
        yield
        q = jnp.concatenate([q_lo, conv_silu(LANES)], axis=1)
        q_ss = _seg_meansq(q, bd16, 1.0)
        yield
        k_lo = conv_silu(2 * LANES)
        yield
        k = jnp.concatenate([k_lo, conv_silu(3 * LANES)], axis=1)
        k_ss = _seg_meansq(k, bd16, 1.0)
        yield
        v_scr[pl.ds(r0, rb), 0:LANES] = conv_silu(4 * LANES)
        yield
        v_scr[pl.ds(r0, rb), LANES:2 * LANES] = conv_silu(5 * LANES)
        ct = ctri_ref[...]
        g_scr[pl.ds(r0, rb), :] = _dot(ct, g_rep[0]) + _dot(ct, g_rep[1]) + _dot(ct, g_rep[2])
        yield
        q_scr[pl.ds(r0, rb), :] = q * lax.rsqrt(q_ss + EPS) * (HEAD_DIM ** -0.5)
        k_scr[pl.ds(r0, rb), :] = k * lax.rsqrt(k_ss + EPS)

    def bd(y16):
        return _tile_rows(y16, N_HEADS) * bd16

    eye = eye_ref[...]

    def wy_stages(i):
        ns = [i * WY_GROUP + c for c in range(WY_GROUP)]
        r0s = [pl.multiple_of(n * CHUNK, CHUNK) for n in ns]
        q = [q_scr[pl.ds(r0, CHUNK), :] for r0 in r0s]
        k = [k_scr[pl.ds(r0, CHUNK), :] for r0 in r0s]
        beta = [beta_scr[pl.ds(r0, CHUNK), :] for r0 in r0s]
        g = [g_scr[pl.ds(r0, CHUNK), :] for r0 in r0s]
        kb = [kc * bc for kc, bc in zip(k, beta)]
        aa = [_dot_nt(jnp.concatenate([kbc.astype(BF16), qc.astype(BF16)], axis=0), bd(kc.astype(BF16)))
              for kbc, qc, kc in zip(kb, q, k)]
        g_row = [_dot01_l(ones_ref[...], gc * eye) for gc in g]
        yield
        decay = [jnp.exp(jnp.minimum(gc - grc, 0.0)) for gc, grc in zip(g, g_row)]
        m = [ac[0:CHUNK] * dc * strict_ref[...] for ac, dc in zip(aa, decay)]
        for r0, ac, dc in zip(r0s, aa, decay):
            aqk_scr[pl.ds(r0, CHUNK), :] = (ac[CHUNK:2 * CHUNK] * dc * causal_ref[...]).astype(BF16)
        pm = [eye - mc for mc in m]
        qm = [_dot(mc.astype(BF16), bd(mc.astype(BF16))) for mc in m]
        yield
        for it in range(5):
            q16 = [qc.astype(BF16) for qc in qm]
            if it < 4:
                r = [_dot(jnp.concatenate([pc.astype(BF16), qc], axis=0), bd(qc)) for pc, qc in zip(pm, q16)]
                pm = [pc + rc[0:CHUNK] for pc, rc in zip(pm, r)]
                qm = [rc[CHUNK:2 * CHUNK] for rc in r]
            else:
                pm = [pc + _dot(pc.astype(BF16), bd(qc)) for pc, qc in zip(pm, q16)]
            yield
        t16 = [pc.astype(BF16) for pc in pm]
        eg = [jnp.exp(gc) for gc in g]
        for c, (n, r0) in enumerate(zip(ns, r0s)):
            v = v_scr[pl.ds(r0, CHUNK), :]
            g_last = g_scr[pl.ds(r0 + CHUNK - 1, 1), :]
            wq_scr[n, 0:CHUNK, :] = _dot(t16[c], bd((kb[c] * eg[c]).astype(BF16))).astype(BF16)
            wq_scr[n, CHUNK:2 * CHUNK, :] = (q[c] * eg[c]).astype(BF16)
            u_scr[pl.ds(r0, CHUNK), :] = _dot(t16[c], bd((v * beta[c]).astype(BF16)))
            kout_scr[pl.ds(r0, CHUNK), :] = (k[c] * jnp.exp(g_last - g[c])).astype(BF16)

    def scan_stages(i):
        st = s_scr[...]
        for c in range(WY_GROUP):
            n = i * WY_GROUP + c
            r0 = pl.multiple_of(n * CHUNK, CHUNK)
            wq = _dot(wq_scr[n], st.astype(BF16))
            yield
            v16 = (u_scr[pl.ds(r0, CHUNK), :] - wq[0:CHUNK]).astype(BF16)
            a_last = jnp.exp(g_scr[pl.ds(r0 + CHUNK - 1, 1), :])
            st = st * a_last + _dot_tn(kout_scr[pl.ds(r0, CHUNK), :], v16) * bdf
            o_scr[pl.ds(r0, CHUNK), :] = wq[CHUNK:2 * CHUNK] + _dot(aqk_scr[pl.ds(r0, CHUNK), :], bd(v16))
            yield
        s_scr[...] = st

    def out_stages(i):
        r0 = pl.multiple_of(i * rb, rb)
        o = o_scr[pl.ds(r0, rb), :]
        o_ms = _seg_meansq(o, bd16, HEAD_DIM)
        yield
        on = o * lax.rsqrt(o_ms + EPS) * og_ref[...]
        o_ref[pl.ds(r0, rb), :] = (on * _silu(z_ref[pl.ds(r0, rb), :])).astype(BF16)

    s_scr[...] = jnp.zeros((GROUP_W, GROUP_W), F32)
    _software_pipeline((prep_stages, wy_stages, scan_stages, out_stages), nblk)


def _gdn(gdn_qkv, gdn_z, gates, cw, alog_row, dt_row, og_row, consts, b, s):
    rb = min(256, s)
    eye, strict, causal = consts["chunk_masks"]
    kern = functools.partial(_gdn_kernel, rb=rb)
    row = lambda w: pl.BlockSpec((s, w), lambda i: (i, 0))
    return pl.pallas_call(
        kern,
        grid=(b,),
        in_specs=[row(3 * GROUP_W), row(GROUP_W), row(GATE_W),
                  _full_spec((8, 3 * GROUP_W)), _full_spec((1, GATE_W)), _full_spec((1, GATE_W)),
                  _full_spec((1, GROUP_W)), _full_spec((GROUP_W, GROUP_W)),
                  _full_spec((GATE_W, GROUP_W)), _full_spec((GATE_W, GROUP_W)),
                  _full_spec((CHUNK, GROUP_W)), _full_spec((CHUNK, GROUP_W)), _full_spec((CHUNK, GROUP_W)),
                  _full_spec((CHUNK, CHUNK)), _full_spec((rb, rb))],
        out_specs=row(GROUP_W),
        out_shape=jax.ShapeDtypeStruct((b * s, GROUP_W), BF16),
        scratch_shapes=[pltpu.VMEM((s + 8, 3 * GROUP_W), F32)]
                       + [pltpu.VMEM((s, GROUP_W), F32) for _ in range(7)]
                       + [pltpu.VMEM((GROUP_W, GROUP_W), F32),
                          pltpu.VMEM((s // CHUNK, 2 * CHUNK, GROUP_W), BF16),
                          pltpu.VMEM((s, GROUP_W), BF16), pltpu.VMEM((s, GROUP_W), BF16)],
        compiler_params=_cparams(1),
        name="gdn",
    )(gdn_qkv, gdn_z, gates, cw, alog_row, dt_row, og_row, consts["bd256"], consts["expand_b"], consts["expand_g"],
      eye, strict, causal, consts["ones64"],
      jnp.asarray(_tril_ones(rb) * _block_mask(rb, CHUNK, rb, CHUNK), BF16))


def _la_kernel(*refs, variant, rb):
    if variant == "hgrn2":
        (x_ref, la_ref, l1_ref, oml_ref, og_ref, bdv_ref, bdk_ref, eye_ref, lvl_ref, ltri_ref,
         o_ref, q_scr, k_scr, a_scr, v_scr, gate_scr, gc_scr, o_scr, st_scr) = refs
        kw, dk = GROUP_W, HEAD_DIM
    else:
        (x_ref, gates_ref, wgk_ref, bgk_ref, og_ref, bdv_ref, bdk_ref, eye_ref, lvl_ref, ltri_ref,
         o_ref, q_scr, k_scr, a_scr, v_scr, gate_scr, gc_scr, o_scr, st_scr) = refs
        kw, dk = GLA_KW, GLA_DK
    s_len = x_ref.shape[0]
    nblk = s_len // rb
    bdv16 = bdv_ref[...]
    bdk16 = bdk_ref[...]
    bdkf = bdk16.astype(F32)

    def prep_stages(i):
        r0 = pl.multiple_of(i * rb, rb)
        if variant == "hgrn2":
            f_logit = x_ref[pl.ds(r0, rb), GROUP_W:2 * GROUP_W]
            q_scr[pl.ds(r0, rb), :] = _silu(x_ref[pl.ds(r0, rb), 0:GROUP_W]) * (dk ** -0.5)
            yield
            a = la_ref[...]
            bterm = l1_ref[...] + _log_sigmoid(f_logit)
            amax = jnp.maximum(a, bterm)
            a_scr[pl.ds(r0, rb), :] = amax + jnp.log1p(jnp.exp(-jnp.abs(a - bterm)))
            yield
            k_scr[pl.ds(r0, rb), :] = oml_ref[...] * jax.nn.sigmoid(-f_logit)
            yield
            v_scr[pl.ds(r0, rb), :] = x_ref[pl.ds(r0, rb), 2 * GROUP_W:3 * GROUP_W]
            gate_scr[pl.ds(r0, rb), :] = x_ref[pl.ds(r0, rb), 3 * GROUP_W:4 * GROUP_W]
        else:
            lr = _dot(gates_ref[pl.ds(r0, rb), :].astype(BF16), wgk_ref[...]) + bgk_ref[...]
            q_scr[pl.ds(r0, rb), :] = x_ref[pl.ds(r0, rb), 0:kw] * (dk ** -0.5)
            k_scr[pl.ds(r0, rb), :] = x_ref[pl.ds(r0, rb), kw:2 * kw]
            yield
            v_scr[pl.ds(r0, rb), :] = x_ref[pl.ds(r0, rb), 2 * kw:2 * kw + GROUP_W]
            gate_scr[pl.ds(r0, rb), :] = x_ref[pl.ds(r0, rb), 2 * kw + GROUP_W:2 * kw + 2 * GROUP_W]
            yield
            a_scr[pl.ds(r0, rb), :] = _log_sigmoid(lr) * (1.0 / GLA_NORM)

    sub = lax.broadcasted_iota(jnp.int32, (8, kw), 0)

    def level_ref(m, slot):
        def brow(r):
            return jnp.broadcast_to(gc_scr[slot, r:r + 1, :], (8, kw))
        pieces = []
        for a in range(CHUNK // 8):
            if 2 * m >= 8:
                pieces.append(brow((8 * a) // (2 * m) * (2 * m) + m - 1))
            elif m == 2:
                pieces.append(jnp.where(sub < 4, brow(8 * a + 1), brow(8 * a + 5)))
            else:
                p = jnp.where(sub < 2, brow(8 * a), brow(8 * a + 2))
                p = jnp.where(sub < 4, p, jnp.where(sub < 6, brow(8 * a + 4), brow(8 * a + 6)))
                pieces.append(p)
        return jnp.concatenate(pieces, axis=0)

    def chunk_stages(i):
        slots = range(WY_GROUP)
        r0s = [pl.multiple_of((i * WY_GROUP + c) * CHUNK, CHUNK) for c in slots]
        q = [q_scr[pl.ds(r0, CHUNK), :] for r0 in r0s]
        k = [k_scr[pl.ds(r0, CHUNK), :] for r0 in r0s]
        v16 = [v_scr[pl.ds(r0, CHUNK), :].astype(BF16) for r0 in r0s]
        g = [_dot01_l(ltri_ref[...], a_scr[pl.ds(r0, CHUNK), :]) for r0 in r0s]
        yield
        for c in slots:
            gc_scr[c] = g[c]
        a_in = [_dot_nt(q[c].astype(BF16), _tile_rows(k[c].astype(BF16), N_HEADS) * bdk16) * eye_ref[...]
                for c in slots]
        yield
        for li, m in enumerate(LEVELS):
            for c in slots:
                e = jnp.exp(-jnp.abs(g[c] - level_ref(m, c)))
                qe = (q[c] * e).astype(BF16)
                ke = (k[c] * e).astype(BF16)
                a_in[c] = a_in[c] + _dot_nt(qe, _tile_rows(ke, N_HEADS) * bdk16) * lvl_ref[li]
            yield
        o = [_dot(a_in[c].astype(BF16), _tile_rows(v16[c], N_HEADS) * bdv16) for c in slots]
        ds = [_dot_tn(v16[c], (k[c] * jnp.exp(g[c][CHUNK - 1:CHUNK, :] - g[c])).astype(BF16)) * bdkf for c in slots]
        q_in = [(q[c] * jnp.exp(g[c])).astype(BF16) for c in slots]
        yield
        st = st_scr[...]
        for c in slots:
            o_scr[pl.ds(r0s[c], CHUNK), :] = o[c] + _dot_nt(q_in[c], st.astype(BF16))
            st = st * jnp.exp(g[c][CHUNK - 1:CHUNK, :]) + ds[c]
        st_scr[...] = st

    def out_stages(i):
        r0 = pl.multiple_of(i * rb, rb)
        o = o_scr[pl.ds(r0, rb), :]
        o_ms = _seg_meansq(o, bdv16, HEAD_DIM)
        yield
        on = o * lax.rsqrt(o_ms + EPS) * og_ref[...]
        o_ref[pl.ds(r0, rb), :] = (on * _silu(gate_scr[pl.ds(r0, rb), :])).astype(BF16)

    st_scr[...] = jnp.zeros((GROUP_W, kw), F32)
    _software_pipeline((prep_stages, chunk_stages, out_stages), nblk)


def _la(variant, x, extra, og_row, consts, b, s):
    rb = min(256, s)
    kw = GROUP_W if variant == "hgrn2" else GLA_KW
    xw = 4 * GROUP_W if variant == "hgrn2" else 3 * GROUP_W
    kern = functools.partial(_la_kernel, variant=variant, rb=rb)
    row = lambda w: pl.BlockSpec((s, w), lambda i: (i, 0))
    if variant == "hgrn2":
        in_specs = [row(xw)] + [_full_spec((1, GROUP_W))] * 3
        bdk = consts["bd256"]
    else:
        in_specs = [row(xw), row(GATE_W), _full_spec((GATE_W, GLA_KW)), _full_spec((1, GLA_KW))]
        bdk = consts["bdk_gla"]
    in_specs += [_full_spec((1, GROUP_W)), _full_spec((GROUP_W, GROUP_W)), _full_spec((GROUP_W, kw)),
                 _full_spec((CHUNK, GROUP_W)), _full_spec((len(LEVELS), CHUNK, GROUP_W)), _full_spec((CHUNK, CHUNK))]
    return pl.pallas_call(
        kern,
        grid=(b,),
        in_specs=in_specs,
        out_specs=row(GROUP_W),
        out_shape=jax.ShapeDtypeStruct((b * s, GROUP_W), BF16),
        scratch_shapes=[pltpu.VMEM((s, kw), F32) for _ in range(3)]
                       + [pltpu.VMEM((s, GROUP_W), F32) for _ in range(2)]
                       + [pltpu.VMEM((WY_GROUP, CHUNK, kw), F32), pltpu.VMEM((s, GROUP_W), F32),
                          pltpu.VMEM((GROUP_W, kw), F32)],
        compiler_params=_cparams(1),
        name="la_" + variant,
    )(x, *extra, og_row, consts["bd256"], bdk, consts["chunk_masks"][0], consts["level_masks"], consts["ltri64"])


FFN_TAIL = 16


def _ffn_kernel(x_ref, oa_ref, ob_ref, oc_ref, od_ref, wo_ref, g2_ref, wup_ref, ctap_ref, wd_ref,
                out_ref, hext, u_scr, act_scr, *, tm):
    it = pl.program_id(1)
    x1 = x_ref[...]
    for kk, o_ref in enumerate((oa_ref, ob_ref, oc_ref, od_ref)):
        x1 = x1 + _dot(o_ref[...], wo_ref[kk])
    out_ref[...] = x1
    ms = jnp.mean(x1 * x1, axis=-1, keepdims=True)
    h2 = (x1 * lax.rsqrt(ms + EPS) * g2_ref[...]).astype(BF16)

    @pl.when(it == 0)
    def _():
        hext[0:FFN_TAIL, :] = jnp.zeros((FFN_TAIL, D_MODEL), BF16)

    hext[FFN_TAIL:, :] = h2

    def conv(col0, slot):
        u_scr[slot] = _dot(hext[...], wup_ref[:, col0:col0 + FF_CHUNK])
        c = ctap_ref[:, col0:col0 + FF_CHUNK]
        y = c[0:1, :] * u_scr[slot, FFN_TAIL - 2:FFN_TAIL - 2 + tm, :]
        y = y + c[1:2, :] * u_scr[slot, FFN_TAIL - 1:FFN_TAIL - 1 + tm, :]
        y = y + c[2:3, :] * u_scr[slot, FFN_TAIL:FFN_TAIL + tm, :]
        return y + c[3:4, :]

    for j in range(D_FF // FF_CHUNK):
        slot = 2 * (j % 2)
        gate = conv(j * FF_CHUNK, slot)
        up = conv(D_FF + j * FF_CHUNK, slot + 1)
        act_scr[:, j * FF_CHUNK:(j + 1) * FF_CHUNK] = (_silu(gate) * up).astype(BF16)
    out_ref[...] += _dot(act_scr[...], wd_ref[...])
    hext[0:FFN_TAIL, :] = hext[tm:tm + FFN_TAIL, :]


def _ffn(x2d, outs, wo, g2, wup, ctap, wd, b, s):
    tm = min(ROW_TILE, s)
    nt = s // tm
    kern = functools.partial(_ffn_kernel, tm=tm)
    row = lambda w: pl.BlockSpec((tm, w), lambda i, j: (i * nt + j, 0))
    return pl.pallas_call(
        kern,
        grid=(b, nt),
        in_specs=[row(D_MODEL)] + [row(GROUP_W)] * 4
                 + [_full_spec((4, GROUP_W, D_MODEL)), _full_spec((1, D_MODEL)),
                    _full_spec((D_MODEL, 2 * D_FF)), _full_spec((8, 2 * D_FF)), _full_spec((D_FF, D_MODEL))],
        out_specs=row(D_MODEL),
        out_shape=jax.ShapeDtypeStruct(x2d.shape, F32),
        scratch_shapes=[pltpu.VMEM((tm + FFN_TAIL, D_MODEL), BF16),
                        pltpu.VMEM((4, tm + FFN_TAIL, FF_CHUNK), F32),
                        pltpu.VMEM((tm, D_FF), BF16)],
        compiler_params=_cparams(2),
        name="outproj_ffn",
    )(x2d, *outs, wo, g2, wup, ctap, wd)


def _constants():
    eye, strict, causal = _chunk_masks()
    return {
        "bd256": jnp.asarray(_block_mask(GROUP_W, HEAD_DIM, GROUP_W, HEAD_DIM), BF16),
        "bdk_gla": jnp.asarray(_block_mask(GROUP_W, HEAD_DIM, GLA_KW, GLA_DK), BF16),
        "expand_b": jnp.asarray(_expand_mat(GATE_GDN_B), BF16),
        "expand_g": jnp.asarray(_expand_mat(GATE_GDN_A), BF16),
        "chunk_masks": (jnp.asarray(eye), jnp.asarray(strict), jnp.asarray(causal)),
        "level_masks": jnp.asarray(_level_masks()),
        "ones64": jnp.ones((CHUNK, CHUNK), BF16),
        "ltri64": jnp.asarray(_tril_ones(CHUNK), BF16),
        "fox_routes": tuple(jnp.asarray(a, BF16) if a.ndim == 2 else jnp.asarray(a) for a in _fox_routes()),
    }


def _w_in_moves():
    sizes = (3 * GROUP_W, N_HEADS, 3 * GROUP_W, N_HEADS, N_HEADS, GROUP_W,
             GROUP_W, GROUP_W, GROUP_W, GROUP_W, 2 * GLA_KW, GROUP_W, GLA_RANK, GROUP_W)
    names = ("fox_qkv", "fox_f", "gdn_qkv", "gdn_b", "gdn_a", "gdn_z", "hg_q", "hg_f", "hg_i", "hg_g",
             "gla_qk", "gla_v", "gla_gk", "gla_g")
    src = dict(zip(names, np.concatenate([[0], np.cumsum(sizes)[:-1]])))
    wid = dict(zip(names, sizes))
    order = ("fox_qkv", "gdn_qkv", "gdn_z", "hg_q", "hg_f", "hg_i", "hg_g", "gla_qk", "gla_v", "gla_g",
             "fox_f", "gdn_b", "gdn_a", "gla_gk")
    moves, dst = [], 0
    for nm in order:
        moves.append((int(src[nm]), dst, wid[nm]))
        dst += wid[nm]
    return moves, dst


def _regroup_kernel(w_ref, o_ref):
    moves, used = _w_in_moves()
    x = w_ref[0]
    for src, dst, wd in moves:
        o_ref[0, :, dst:dst + wd] = x[:, src:src + wd].astype(BF16)
    o_ref[0, :, used:IN_COLS_PAD] = jnp.zeros((x.shape[0], IN_COLS_PAD - used), BF16)


def _regroup_w_in(w_in):
    depth, d, cols = w_in.shape
    tr = 256
    return pl.pallas_call(
        _regroup_kernel,
        grid=(depth, d // tr),
        in_specs=[pl.BlockSpec((1, tr, cols), lambda l, i: (l, i, 0))],
        out_specs=pl.BlockSpec((1, tr, IN_COLS_PAD), lambda l, i: (l, i, 0)),
        out_shape=jax.ShapeDtypeStruct((depth, d, IN_COLS_PAD), BF16),
        compiler_params=_cparams(2),
        name="regroup_w_in",
    )(w_in)


def _lane_row(vals, lane0, width):
    return jnp.zeros((1, width), F32).at[0, lane0:lane0 + vals.shape[0]].set(vals.astype(F32))


def _tile_heads(g):
    return jnp.tile(g.astype(F32), N_HEADS)[None, :]


def kernel(x, norm1_g, w_in, fox_qn_g, fox_kn_g, fox_b_f, fox_on_g, gdn_conv_w, gdn_a_log, gdn_dt_bias, gdn_on_g,
           hg_lb, hg_on_g, gla_w_gk, gla_b_gk, gla_on_g, w_out, norm2_g, w_up, ffn_conv_w, ffn_conv_b, w_down):
    b, s, d = x.shape
    depth = w_in.shape[0]
    consts = _constants()

    cs = jnp.cumsum(jax.nn.softmax(hg_lb.astype(F32), axis=0), axis=0)
    lower = cs - cs[0:1]

    w_in_all = _regroup_w_in(w_in)
    x2d = x.reshape(b * s, d)
    for l in range(depth):
        fox_qkv, gdn_qkv, gdn_z, hg, gla, gates = _inproj(x2d, norm1_g[l][None, :], w_in_all, l)

        qa, ka, va = _fox_prep(fox_qkv, gates, _lane_row(fox_b_f[l], GATE_FOX_F, GATE_W),
                               _tile_heads(fox_qn_g[l]), _tile_heads(fox_kn_g[l]), consts, b, s)
        o_a = _fox_attn(qa, ka, va, jnp.tile(fox_on_g[l].astype(F32), 2)[None, :], b, s)

        cw = jnp.zeros((8, 3 * GROUP_W), F32).at[0:GDN_CONV].set(gdn_conv_w[l])
        o_b = _gdn(gdn_qkv, gdn_z, gates, cw, _lane_row(gdn_a_log[l], GATE_GDN_A, GATE_W),
                   _lane_row(gdn_dt_bias[l], GATE_GDN_A, GATE_W), _tile_heads(gdn_on_g[l]), consts, b, s)

        lb = lower[l][None, :]
        o_c = _la("hgrn2", hg, (jnp.log(lb), jnp.log1p(-lb), 1.0 - lb), _tile_heads(hg_on_g[l]), consts, b, s)

        wgk = jnp.zeros((GATE_W, GLA_KW), F32).at[GATE_GLA_GK:GATE_GLA_GK + GLA_RANK].set(gla_w_gk[l]).astype(BF16)
        o_d = _la("gla", gla, (gates, wgk, gla_b_gk[l][None, :].astype(F32)), _tile_heads(gla_on_g[l]), consts, b, s)

        wo = w_out[l].reshape(4, GROUP_W, d).astype(BF16)
        taps = jnp.concatenate([ffn_conv_w[l], ffn_conv_b[l][None, :],
                                jnp.zeros((8 - FFN_CONV - 1, 2 * D_FF), F32)], axis=0)
        x2d = _ffn(x2d, (o_a, o_b, o_c, o_d), wo, norm2_g[l][None, :], w_up[l].astype(BF16), taps,
                   w_down[l].astype(BF16), b, s)
    return x2d.reshape(b, s, d)
```

```python
import functools

import numpy as np
import jax
import jax.numpy as jnp
from jax import lax
from jax.experimental import pallas as pl
from jax.experimental.pallas import tpu as pltpu

F32 = jnp.float32
BF16 = jnp.bfloat16

D_MODEL = 1024
N_HEADS = 4
HEAD_DIM = 64
GROUP_W = N_HEADS * HEAD_DIM
GLA_DK = 32
GLA_KW = N_HEADS * GLA_DK
GLA_RANK = 16
GLA_NORM = 16.0
GDN_CONV = 4
D_FF = 2816
FFN_CONV = 3
EPS = 1e-6
CHUNK = 64
WY_GROUP = 4

LANES = 128
GATE_W = LANES
GATE_FOX_F, GATE_GDN_B, GATE_GDN_A, GATE_GLA_GK = 0, 4, 8, 12

IN_SEGS = (3 * GROUP_W, 3 * GROUP_W, GROUP_W, 4 * GROUP_W, 3 * GROUP_W, GATE_W)
IN_COLS_PAD = sum(IN_SEGS)

ROW_TILE = 512
FFN_ROW_TILE = 512
FF_CHUNK = 256
FOX_BLOCK = 512
NEG_BIG = -1e30
LOG2_E = 1.4426950408889634

VMEM_LIMIT = 56 * 1024 * 1024


def _dot(a, b):
    return jnp.dot(a, b, preferred_element_type=F32)


def _dot_nt(a, b):
    return lax.dot_general(a, b, (((1,), (1,)), ((), ())), preferred_element_type=F32)


def _dot_tn(a, b):
    return lax.dot_general(a, b, (((0,), (0,)), ((), ())), preferred_element_type=F32)


def _split3(x):
    hi = x.astype(BF16)
    r1 = x - hi.astype(F32)
    mid = r1.astype(BF16)
    lo = (r1 - mid.astype(F32)).astype(BF16)
    return hi, mid, lo


def _dot01_l(m01, x):
    hi, mid, lo = _split3(x)
    return _dot(m01, hi) + _dot(m01, mid) + _dot(m01, lo)


def _dot01_r(x, m01):
    hi, mid, lo = _split3(x)
    return _dot(hi, m01) + _dot(mid, m01) + _dot(lo, m01)


def _seg_meansq(x, bd16, width):
    sq = x * x
    hi = sq.astype(BF16)
    lo = (sq - hi.astype(F32)).astype(BF16)
    return (_dot(hi, bd16) + _dot(lo, bd16)) * (1.0 / width)


def _log_sigmoid(x):
    return jnp.minimum(x, 0.0) - jnp.log1p(jnp.exp(-jnp.abs(x)))


def _softplus(x):
    return jnp.maximum(x, 0.0) + jnp.log1p(jnp.exp(-jnp.abs(x)))


def _silu(x):
    return x * jax.nn.sigmoid(x)


def _tile_rows(x, n):
    return jnp.concatenate([x] * n, axis=0)


def _interleave(*gens):
    live = list(gens)
    while live:
        for gen in list(live):
            try:
                next(gen)
            except StopIteration:
                live.remove(gen)


def _software_pipeline(phases, n_steps):
    def step(t):
        _interleave(*[ph(t - d) for d, ph in enumerate(phases) if isinstance(t, jax.Array) or 0 <= t - d < n_steps])

    depth = len(phases) - 1
    for t in range(n_steps + depth):
        if depth <= t < n_steps:
            if t == depth:
                lax.fori_loop(depth, n_steps, lambda tt, c: (step(tt), c)[1], 0)
        else:
            step(t)


def _full_spec(shape):
    nd = len(shape)
    return pl.BlockSpec(shape, lambda *_: (0,) * nd, pipeline_mode=pl.Buffered(1))


def _cparams(n_axes, flags=None):
    return pltpu.CompilerParams(dimension_semantics=("arbitrary",) * n_axes, vmem_limit_bytes=VMEM_LIMIT,
                                flags=flags)


def _block_mask(rows, row_blk, cols, col_blk):
    r = np.arange(rows)[:, None] // row_blk
    c = np.arange(cols)[None, :] // col_blk
    return (r == c).astype(np.float32)


def _tril_ones(n):
    return np.tril(np.ones((n, n), np.float32))


def _expand_mat(lane0):
    m = np.zeros((GATE_W, GROUP_W), np.float32)
    for h in range(N_HEADS):
        m[lane0 + h, h * HEAD_DIM:(h + 1) * HEAD_DIM] = 1.0
    return m


def _chunk_masks():
    r = np.arange(CHUNK)[:, None]
    s = np.arange(GROUP_W)[None, :] % CHUNK
    return ((r == s).astype(np.float32), (r > s).astype(np.float32), (r >= s).astype(np.float32))


LEVELS = (1, 2, 4, 8, 16, 32)


def _level_masks():
    r = np.arange(CHUNK)[:, None]
    s = np.arange(GROUP_W)[None, :] % CHUNK
    out = []
    for m in LEVELS:
        same = (r // (2 * m)) == (s // (2 * m))
        out.append((same & ((r % (2 * m)) >= m) & ((s % (2 * m)) < m)).astype(np.float32))
    return np.stack(out)


def _fox_routes():
    rq = np.zeros((3 * GATE_W, N_HEADS * LANES), np.float32)
    rk = np.zeros((3 * GATE_W, N_HEADS * LANES), np.float32)
    oq = np.zeros((N_HEADS, 1, LANES), np.float32)
    ok = np.zeros((N_HEADS, 1, LANES), np.float32)
    for h in range(N_HEADS):
        base = (1 - h % 2) * HEAD_DIM
        for j in range(3):
            rq[j * GATE_W + GATE_FOX_F + h, h * LANES + base + j] = 1.0
            rk[j * GATE_W + GATE_FOX_F + h, h * LANES + base + 3 + j] = -1.0
            oq[h, 0, base + 3 + j] = 1.0
            ok[h, 0, base + j] = 1.0
    return rq, rk, oq, ok


def _inproj_kernel(x_ref, g_ref, w_ref, *out_refs):
    x = x_ref[...]
    ms = jnp.mean(x * x, axis=-1, keepdims=True)
    h = (x * lax.rsqrt(ms + EPS) * g_ref[...]).astype(BF16)
    off = 0
    for ref, width in zip(out_refs, IN_SEGS):
        step = min(width, 256)
        for c in range(0, width, step):
            ref[:, c:c + step] = _dot(h, w_ref[:, off + c:off + c + step])
        off += width


def _inproj(x2d, g, w_all, layer):
    t = x2d.shape[0]
    tm = min(ROW_TILE, t)
    return pl.pallas_call(
        _inproj_kernel,
        grid=(t // tm,),
        in_specs=[pl.BlockSpec((tm, D_MODEL), lambda i: (i, 0)),
                  _full_spec((1, D_MODEL)),
                  pl.BlockSpec((None, D_MODEL, IN_COLS_PAD), lambda i: (layer, 0, 0), pipeline_mode=pl.Buffered(1))],
        out_specs=[pl.BlockSpec((tm, wd), lambda i: (i, 0)) for wd in IN_SEGS],
        out_shape=[jax.ShapeDtypeStruct((t, wd), F32) for wd in IN_SEGS],
        compiler_params=_cparams(1),
        name="inproj",
    )(x2d, g, w_all)


def _fox_prep_kernel(qkv_ref, gates_ref, bf_ref, qg_ref, kg_ref, bd_ref, ltri_ref, rq_ref, rk_ref, oq_ref, ok_ref,
                     qa_ref, ka_ref, va_ref, *, rb):
    s_len = qkv_ref.shape[0]
    lane = lax.broadcasted_iota(jnp.int32, (rb, LANES), 1)

    def blk(i, carry):
        r0 = pl.multiple_of(i * rb, rb)
        logf = _log_sigmoid(gates_ref[pl.ds(r0, rb), :] + bf_ref[...])
        c = _dot01_l(ltri_ref[...], logf) + carry
        c3 = jnp.concatenate(_split3(c), axis=1)
        qkv = qkv_ref[pl.ds(r0, rb), :]
        q = qkv[:, 0:GROUP_W]
        k = qkv[:, GROUP_W:2 * GROUP_W]
        v = qkv[:, 2 * GROUP_W:3 * GROUP_W]
        bd = bd_ref[...]
        qn = q * lax.rsqrt(_seg_meansq(q, bd, HEAD_DIM) + EPS) * qg_ref[...] * (HEAD_DIM ** -0.5)
        kn = k * lax.rsqrt(_seg_meansq(k, bd, HEAD_DIM) + EPS) * kg_ref[...]
        q_route = _dot(c3, rq_ref[...])
        k_route = _dot(c3, rk_ref[...])
        for h in range(N_HEADS):
            p = h // 2
            own = (lane // HEAD_DIM) == (h % 2)
            qa = jnp.where(own, qn[:, p * LANES:(p + 1) * LANES], q_route[:, h * LANES:(h + 1) * LANES] + oq_ref[h])
            ka = jnp.where(own, kn[:, p * LANES:(p + 1) * LANES], k_route[:, h * LANES:(h + 1) * LANES] + ok_ref[h])
            va = jnp.where(own, v[:, p * LANES:(p + 1) * LANES], jnp.where(lane == (1 - h % 2) * HEAD_DIM, 1.0, 0.0))
            qa_ref[0, h, pl.ds(r0, rb), :] = qa.astype(BF16)
            ka_ref[0, h, pl.ds(r0, rb), :] = ka.astype(BF16)
            va_ref[0, h, pl.ds(r0, rb), :] = va.astype(BF16)
        return c[rb - 1:rb, :]

    lax.fori_loop(0, s_len // rb, blk, jnp.zeros((1, GATE_W), F32))


def _fox_prep(fox_qkv, gates, bf_row, qg_row, kg_row, consts, b, s):
    rb = min(256, s)
    rq, rk, oq, ok = consts["fox_routes"]
    ltri = jnp.asarray(_tril_ones(rb), BF16)
    kern = functools.partial(_fox_prep_kernel, rb=rb)
    return pl.pallas_call(
        kern,
        grid=(b,),
        in_specs=[pl.BlockSpec((s, 3 * GROUP_W), lambda i: (i, 0)),
                  pl.BlockSpec((s, GATE_W), lambda i: (i, 0)),
                  _full_spec((1, GATE_W)), _full_spec((1, GROUP_W)), _full_spec((1, GROUP_W)),
                  _full_spec((GROUP_W, GROUP_W)), _full_spec((rb, rb)),
                  _full_spec(rq.shape), _full_spec(rk.shape), _full_spec(oq.shape), _full_spec(ok.shape)],
        out_specs=[pl.BlockSpec((1, N_HEADS, s, LANES), lambda i: (i, 0, 0, 0)),
                   pl.BlockSpec((1, N_HEADS, s, LANES), lambda i: (i, 0, 0, 0)),
                   pl.BlockSpec((1, N_HEADS, s, LANES), lambda i: (i, 0, 0, 0))],
        out_shape=[jax.ShapeDtypeStruct((b, N_HEADS, s, LANES), BF16)] * 3,
        compiler_params=_cparams(1),
        name="fox_prep",
    )(fox_qkv, gates, bf_row, qg_row, kg_row, consts["bd256"], ltri, rq, rk, oq, ok)


def _fox_attn_kernel(q_ref, k_ref, v_ref, og_ref, o_ref, s_scr, m_scr, acc_scr, *, blk, nq):
    lane = lax.broadcasted_iota(jnp.int32, (blk, LANES), 1)
    row = lax.broadcasted_iota(jnp.int32, (blk, blk), 0)
    col = lax.broadcasted_iota(jnp.int32, (blk, blk), 1)

    def fold(s):
        m = s[:, 0:LANES]
        for c in range(1, blk // LANES):
            m = jnp.maximum(m, s[:, c * LANES:(c + 1) * LANES])
        return m

    def logit_stages(i):
        slot = i % 2
        for j in range(i + 1):
            for hh in range(2):
                s = _dot_nt(q_ref[0, hh, i * blk:(i + 1) * blk, :], k_ref[0, hh, j * blk:(j + 1) * blk, :])
                if j == i:
                    s = jnp.where(col <= row, s, NEG_BIG)
                s_scr[slot, hh, j] = s
                m_scr[slot, hh] = fold(s) if j == 0 else jnp.maximum(m_scr[slot, hh], fold(s))
            yield

    def value_stages(i):
        slot = i % 2
        row_max = [jnp.max(m_scr[slot, hh], axis=-1, keepdims=True) for hh in range(2)]
        for j in range(i + 1):
            for hh in range(2):
                p = jnp.exp(s_scr[slot, hh, j] - row_max[hh])
                pv = _dot(p.astype(BF16), v_ref[0, hh, j * blk:(j + 1) * blk, :])
                acc_scr[slot, hh] = pv if j == 0 else acc_scr[slot, hh] + pv
            yield
        outs = []
        for hh in range(2):
            acc = acc_scr[slot, hh]
            l = jnp.sum(jnp.where(lane == (1 - hh) * HEAD_DIM, acc, 0.0), axis=-1, keepdims=True)
            o = acc / l
            own = (lane // HEAD_DIM) == hh
            ms = jnp.sum(jnp.where(own, o * o, 0.0), axis=-1, keepdims=True) * (1.0 / HEAD_DIM)
            outs.append(o * lax.rsqrt(ms + EPS) * og_ref[...])
        o_ref[i * blk:(i + 1) * blk, :] = jnp.where((lane // HEAD_DIM) == 0, outs[0], outs[1]).astype(BF16)

    _interleave(logit_stages(0))
    for i in range(nq):
        if i + 1 < nq:
            _interleave(value_stages(i), logit_stages(i + 1))
        else:
            _interleave(value_stages(i))


def _fox_attn(qa, ka, va, og_row, b, s):
    blk = min(FOX_BLOCK, s)
    nq = s // blk
    kern = functools.partial(_fox_attn_kernel, blk=blk, nq=nq)
    head_pair = pl.BlockSpec((1, 2, s, LANES), lambda i, p: (i, p, 0, 0))
    return pl.pallas_call(
        kern,
        grid=(b, 2),
        in_specs=[head_pair, head_pair, head_pair, _full_spec((1, LANES))],
        out_specs=pl.BlockSpec((s, LANES), lambda i, p: (i, p)),
        out_shape=jax.ShapeDtypeStruct((b * s, GROUP_W), BF16),
        scratch_shapes=[pltpu.VMEM((2, 2, nq, blk, blk), F32),
                        pltpu.VMEM((2, 2, blk, LANES), F32),
                        pltpu.VMEM((2, 2, blk, LANES), F32)],
        compiler_params=_cparams(2),
        name="fox_attn",
    )(qa, ka, va, og_row)


def _gdn_kernel(qkv_ref, z_ref, gates_ref, cw_ref, alog_ref, dt_ref, og_ref, bd_ref, eb_ref, eg_ref,
                eye_ref, strict_ref, causal_ref, ones_ref, ctri_ref,
                o_ref, xpad, q_scr, k_scr, v_scr, beta_scr, g_scr, o_scr, u_scr, s_scr,
                wq_scr, aqk_scr, kout_scr, *, rb):
    s_len = qkv_ref.shape[0]
    nblk = s_len // rb
    bd16 = bd_ref[...]
    bdf = bd16.astype(F32)

    xpad[0:8, :] = jnp.zeros((8, 3 * GROUP_W), F32)
    xpad[8:, :] = qkv_ref[...]

    def prep_stages(i):
        r0 = pl.multiple_of(i * rb, rb)

        gt = gates_ref[pl.ds(r0, rb), :]
        beta_scr[pl.ds(r0, rb), :] = _dot01_r(jax.nn.sigmoid(gt), eb_ref[...])
        g_s = -jnp.exp(alog_ref[...]) * _softplus(gt + dt_ref[...])
        g_rep = [_dot(t, eg_ref[...]).astype(BF16) for t in _split3(g_s)]

        def conv_silu(c0):
            xx = xpad[pl.ds(r0, rb + 8), c0:c0 + LANES]
            y = cw_ref[GDN_CONV - 1:GDN_CONV, c0:c0 + LANES] * xx[8:8 + rb]
            for j in range(GDN_CONV - 1):
                y = y + cw_ref[j:j + 1, c0:c0 + LANES] * pltpu.roll(xx, GDN_CONV - 1 - j, 0)[8:8 + rb]
            return _silu(y)

        yield
        q_lo = conv_silu(0)
        yield
        q = jnp.concatenate([q_lo, conv_silu(LANES)], axis=1)
        q_ss = _seg_meansq(q, bd16, 1.0)
        yield
        k_lo = conv_silu(2 * LANES)
        yield
        k = jnp.concatenate([k_lo, conv_silu(3 * LANES)], axis=1)
        k_ss = _seg_meansq(k, bd16, 1.0)
        yield
        v_scr[pl.ds(r0, rb), 0:LANES] = conv_silu(4 * LANES)
        yield
        v_scr[pl.ds(r0, rb), LANES:2 * LANES] = conv_silu(5 * LANES)
        ct = ctri_ref[...]
        g_scr[pl.ds(r0, rb), :] = _dot(ct, g_rep[0]) + _dot(ct, g_rep[1]) + _dot(ct, g_rep[2])
        yield
        q_scr[pl.ds(r0, rb), :] = q * lax.rsqrt(q_ss + EPS) * (HEAD_DIM ** -0.5)
        k_scr[pl.ds(r0, rb), :] = k * lax.rsqrt(k_ss + EPS)

    def bd(y16):
        return _tile_rows(y16, N_HEADS) * bd16

    eye = eye_ref[...]

    def wy_stages(i):
        ns = [i * WY_GROUP + c for c in range(WY_GROUP)]
        r0s = [pl.multiple_of(n * CHUNK, CHUNK) for n in ns]
        q = [q_scr[pl.ds(r0, CHUNK), :] for r0 in r0s]
        k = [k_scr[pl.ds(r0, CHUNK), :] for r0 in r0s]
        beta = [beta_scr[pl.ds(r0, CHUNK), :] for r0 in r0s]
        g = [g_scr[pl.ds(r0, CHUNK), :] for r0 in r0s]
        kb = [kc * bc for kc, bc in zip(k, beta)]
        aa = [_dot_nt(jnp.concatenate([kbc.astype(BF16), qc.astype(BF16)], axis=0), bd(kc.astype(BF16)))
              for kbc, qc, kc in zip(kb, q, k)]
        g_row = [_dot01_l(ones_ref[...], gc * eye) for gc in g]
        yield
        decay = [jnp.exp(jnp.minimum(gc - grc, 0.0)) for gc, grc in zip(g, g_row)]
        m = [ac[0:CHUNK] * dc * strict_ref[...] for ac, dc in zip(aa, decay)]
        for r0, ac, dc in zip(r0s, aa, decay):
            aqk_scr[pl.ds(r0, CHUNK), :] = (ac[CHUNK:2 * CHUNK] * dc * causal_ref[...]).astype(BF16)
        pm = [eye - mc for mc in m]
        qm = [_dot(mc.astype(BF16), bd(mc.astype(BF16))) for mc in m]
        yield
        for it in range(5):
            q16 = [qc.astype(BF16) for qc in qm]
            if it < 4:
                r = [_dot(jnp.concatenate([pc.astype(BF16), qc], axis=0), bd(qc)) for pc, qc in zip(pm, q16)]
                pm = [pc + rc[0:CHUNK] for pc, rc in zip(pm, r)]
                qm = [rc[CHUNK:2 * CHUNK] for rc in r]
            else:
                pm = [pc + _dot(pc.astype(BF16), bd(qc)) for pc, qc in zip(pm, q16)]
            yield
        t16 = [pc.astype(BF16) for pc in pm]
        eg = [jnp.exp(gc) for gc in g]
        for c, (n, r0) in enumerate(zip(ns, r0s)):
            v = v_scr[pl.ds(r0, CHUNK), :]
            g_last = g_scr[pl.ds(r0 + CHUNK - 1, 1), :]
            wq_scr[n, 0:CHUNK, :] = _dot(t16[c], bd((kb[c] * eg[c]).astype(BF16))).astype(BF16)
            wq_scr[n, CHUNK:2 * CHUNK, :] = (q[c] * eg[c]).astype(BF16)
            u_scr[pl.ds(r0, CHUNK), :] = _dot(t16[c], bd((v * beta[c]).astype(BF16)))
            kout_scr[pl.ds(r0, CHUNK), :] = (k[c] * jnp.exp(g_last - g[c])).astype(BF16)

    def scan_stages(i):
        st = s_scr[...]
        for c in range(WY_GROUP):
            n = i * WY_GROUP + c
            r0 = pl.multiple_of(n * CHUNK, CHUNK)
            wq = _dot(wq_scr[n], st.astype(BF16))
            yield
            v16 = (u_scr[pl.ds(r0, CHUNK), :] - wq[0:CHUNK]).astype(BF16)
            a_last = jnp.exp(g_scr[pl.ds(r0 + CHUNK - 1, 1), :])
            st = st * a_last + _dot_tn(kout_scr[pl.ds(r0, CHUNK), :], v16) * bdf
            o_scr[pl.ds(r0, CHUNK), :] = wq[CHUNK:2 * CHUNK] + _dot(aqk_scr[pl.ds(r0, CHUNK), :], bd(v16))
            yield
        s_scr[...] = st

    def out_stages(i):
        r0 = pl.multiple_of(i * rb, rb)
        o = o_scr[pl.ds(r0, rb), :]
        o_ms = _seg_meansq(o, bd16, HEAD_DIM)
        yield
        on = o * lax.rsqrt(o_ms + EPS) * og_ref[...]
        o_ref[pl.ds(r0, rb), :] = (on * _silu(z_ref[pl.ds(r0, rb), :])).astype(BF16)

    s_scr[...] = jnp.zeros((GROUP_W, GROUP_W), F32)
    _software_pipeline((prep_stages, wy_stages, scan_stages, out_stages), nblk)


def _gdn(gdn_qkv, gdn_z, gates, cw, alog_row, dt_row, og_row, consts, b, s):
    rb = min(256, s)
    eye, strict, causal = consts["chunk_masks"]
    kern = functools.partial(_gdn_kernel, rb=rb)
    row = lambda w: pl.BlockSpec((s, w), lambda i: (i, 0))
    return pl.pallas_call(
        kern,
        grid=(b,),
        in_specs=[row(3 * GROUP_W), row(GROUP_W), row(GATE_W),
                  _full_spec((8, 3 * GROUP_W)), _full_spec((1, GATE_W)), _full_spec((1, GATE_W)),
                  _full_spec((1, GROUP_W)), _full_spec((GROUP_W, GROUP_W)),
                  _full_spec((GATE_W, GROUP_W)), _full_spec((GATE_W, GROUP_W)),
                  _full_spec((CHUNK, GROUP_W)), _full_spec((CHUNK, GROUP_W)), _full_spec((CHUNK, GROUP_W)),
                  _full_spec((CHUNK, CHUNK)), _full_spec((rb, rb))],
        out_specs=row(GROUP_W),
        out_shape=jax.ShapeDtypeStruct((b * s, GROUP_W), BF16),
        scratch_shapes=[pltpu.VMEM((s + 8, 3 * GROUP_W), F32)]
                       + [pltpu.VMEM((s, GROUP_W), F32) for _ in range(7)]
                       + [pltpu.VMEM((GROUP_W, GROUP_W), F32),
                          pltpu.VMEM((s // CHUNK, 2 * CHUNK, GROUP_W), BF16),
                          pltpu.VMEM((s, GROUP_W), BF16), pltpu.VMEM((s, GROUP_W), BF16)],
        compiler_params=_cparams(1),
        name="gdn",
    )(gdn_qkv, gdn_z, gates, cw, alog_row, dt_row, og_row, consts["bd256"], consts["expand_b"], consts["expand_g"],
      eye, strict, causal, consts["ones64"],
      jnp.asarray(_tril_ones(rb) * _block_mask(rb, CHUNK, rb, CHUNK), BF16))


def _la_kernel(*refs, variant, rb):
    if variant == "hgrn2":
        (x_ref, la_ref, l1_ref, oml_ref, og_ref, bdv_ref, bdk_ref, eye_ref, lvl_ref, ltri_ref,
         o_ref, q_scr, k_scr, a_scr, v_scr, gate_scr, gc_scr, o_scr, st_scr) = refs
        kw, dk = GROUP_W, HEAD_DIM
    else:
        (x_ref, gates_ref, wgk_ref, bgk_ref, og_ref, bdv_ref, bdk_ref, eye_ref, lvl_ref, ltri_ref,
         o_ref, q_scr, k_scr, a_scr, v_scr, gate_scr, gc_scr, o_scr, st_scr) = refs
        kw, dk = GLA_KW, GLA_DK
    s_len = x_ref.shape[0]
    nblk = s_len // rb
    bdv16 = bdv_ref[...]
    bdk16 = bdk_ref[...]
    bdkf = bdk16.astype(F32)

    def prep_stages(i):
        r0 = pl.multiple_of(i * rb, rb)
        if variant == "hgrn2":
            f_logit = x_ref[pl.ds(r0, rb), GROUP_W:2 * GROUP_W]
            q_scr[pl.ds(r0, rb), :] = _silu(x_ref[pl.ds(r0, rb), 0:GROUP_W]) * (dk ** -0.5)
            yield
            a = la_ref[...]
            bterm = l1_ref[...] + _log_sigmoid(f_logit)
            amax = jnp.maximum(a, bterm)
            a_scr[pl.ds(r0, rb), :] = amax + jnp.log1p(jnp.exp(-jnp.abs(a - bterm)))
            yield
            k_scr[pl.ds(r0, rb), :] = oml_ref[...] * jax.nn.sigmoid(-f_logit)
            yield
            v_scr[pl.ds(r0, rb), :] = x_ref[pl.ds(r0, rb), 2 * GROUP_W:3 * GROUP_W]
            gate_scr[pl.ds(r0, rb), :] = x_ref[pl.ds(r0, rb), 3 * GROUP_W:4 * GROUP_W]
        else:
            lr = _dot(gates_ref[pl.ds(r0, rb), :].astype(BF16), wgk_ref[...]) + bgk_ref[...]
            q_scr[pl.ds(r0, rb), :] = x_ref[pl.ds(r0, rb), 0:kw] * (dk ** -0.5)
            k_scr[pl.ds(r0, rb), :] = x_ref[pl.ds(r0, rb), kw:2 * kw]
            yield
            v_scr[pl.ds(r0, rb), :] = x_ref[pl.ds(r0, rb), 2 * kw:2 * kw + GROUP_W]
            gate_scr[pl.ds(r0, rb), :] = x_ref[pl.ds(r0, rb), 2 * kw + GROUP_W:2 * kw + 2 * GROUP_W]
            yield
            a_scr[pl.ds(r0, rb), :] = _log_sigmoid(lr) * (1.0 / GLA_NORM)

    sub = lax.broadcasted_iota(jnp.int32, (8, kw), 0)

    def level_ref(m, slot):
        def brow(r):
            return jnp.broadcast_to(gc_scr[slot, r:r + 1, :], (8, kw))
        pieces = []
        for a in range(CHUNK // 8):
            if 2 * m >= 8:
                pieces.append(brow((8 * a) // (2 * m) * (2 * m) + m - 1))
            elif m == 2:
                pieces.append(jnp.where(sub < 4, brow(8 * a + 1), brow(8 * a + 5)))
            else:
                p = jnp.where(sub < 2, brow(8 * a), brow(8 * a + 2))
                p = jnp.where(sub < 4, p, jnp.where(sub < 6, brow(8 * a + 4), brow(8 * a + 6)))
                pieces.append(p)
        return jnp.concatenate(pieces, axis=0)

    def chunk_stages(i):
        slots = range(WY_GROUP)
        r0s = [pl.multiple_of((i * WY_GROUP + c) * CHUNK, CHUNK) for c in slots]
        q = [q_scr[pl.ds(r0, CHUNK), :] for r0 in r0s]
        k = [k_scr[pl.ds(r0, CHUNK), :] for r0 in r0s]
        v16 = [v_scr[pl.ds(r0, CHUNK), :].astype(BF16) for r0 in r0s]
        g = [_dot01_l(ltri_ref[...], a_scr[pl.ds(r0, CHUNK), :]) * LOG2_E for r0 in r0s]
        yield
        for c in slots:
            gc_scr[c] = g[c]
        q16 = [qc.astype(BF16) for qc in q]
        k16 = [kc.astype(BF16) for kc in k]
        a_in = [_dot_nt(q16[c], _tile_rows(k16[c], N_HEADS) * bdk16) * eye_ref[...] for c in slots]
        yield
        for li, m in enumerate(LEVELS):
            for c in slots:
                e = jnp.exp2(-jnp.abs(g[c] - level_ref(m, c))).astype(BF16)
                a_in[c] = a_in[c] + _dot_nt(q16[c] * e, _tile_rows(k16[c] * e, N_HEADS) * bdk16) * lvl_ref[li]
            yield
        o = [_dot(a_in[c].astype(BF16), _tile_rows(v16[c], N_HEADS) * bdv16) for c in slots]
        ds = [_dot_tn(v16[c], (k[c] * jnp.exp2(g[c][CHUNK - 1:CHUNK, :] - g[c])).astype(BF16)) * bdkf for c in slots]
        q_in = [(q[c] * jnp.exp2(g[c])).astype(BF16) for c in slots]
        yield
        st = st_scr[...]
        for c in slots:
            o_scr[pl.ds(r0s[c], CHUNK), :] = o[c] + _dot_nt(q_in[c], st.astype(BF16))
            st = st * jnp.exp2(g[c][CHUNK - 1:CHUNK, :]) + ds[c]
        st_scr[...] = st

    def out_stages(i):
        r0 = pl.multiple_of(i * rb, rb)
        o = o_scr[pl.ds(r0, rb), :]
        o_ms = _seg_meansq(o, bdv16, HEAD_DIM)
        yield
        on = o * lax.rsqrt(o_ms + EPS) * og_ref[...]
        o_ref[pl.ds(r0, rb), :] = (on * _silu(gate_scr[pl.ds(r0, rb), :])).astype(BF16)

    st_scr[...] = jnp.zeros((GROUP_W, kw), F32)
    _software_pipeline((prep_stages, chunk_stages, out_stages), nblk)


def _la(variant, x, extra, og_row, consts, b, s):
    rb = min(256, s)
    kw = GROUP_W if variant == "hgrn2" else GLA_KW
    xw = 4 * GROUP_W if variant == "hgrn2" else 3 * GROUP_W
    kern = functools.partial(_la_kernel, variant=variant, rb=rb)
    row = lambda w: pl.BlockSpec((s, w), lambda i: (i, 0))
    if variant == "hgrn2":
        in_specs = [row(xw)] + [_full_spec((1, GROUP_W))] * 3
        bdk = consts["bd256"]
    else:
        in_specs = [row(xw), row(GATE_W), _full_spec((GATE_W, GLA_KW)), _full_spec((1, GLA_KW))]
        bdk = consts["bdk_gla"]
    in_specs += [_full_spec((1, GROUP_W)), _full_spec((GROUP_W, GROUP_W)), _full_spec((GROUP_W, kw)),
                 _full_spec((CHUNK, GROUP_W)), _full_spec((len(LEVELS), CHUNK, GROUP_W)), _full_spec((CHUNK, CHUNK))]
    return pl.pallas_call(
        kern,
        grid=(b,),
        in_specs=in_specs,
        out_specs=row(GROUP_W),
        out_shape=jax.ShapeDtypeStruct((b * s, GROUP_W), BF16),
        scratch_shapes=[pltpu.VMEM((s, kw), F32) for _ in range(3)]
                       + [pltpu.VMEM((s, GROUP_W), F32) for _ in range(2)]
                       + [pltpu.VMEM((WY_GROUP, CHUNK, kw), F32), pltpu.VMEM((s, GROUP_W), F32),
                          pltpu.VMEM((GROUP_W, kw), F32)],
        compiler_params=_cparams(1),
        name="la_" + variant,
    )(x, *extra, og_row, consts["bd256"], bdk, consts["chunk_masks"][0], consts["level_masks"], consts["ltri64"])


FFN_TAIL = 16
FFN_LATE_CHUNKS = 2


def _ffn_kernel(x_ref, oa_ref, ob_ref, oc_ref, od_ref, wo_ref, g2_ref, wup_ref, ctap_ref, wd_ref,
                out_ref, hext, act_scr, *, tm):
    it = pl.program_id(1)
    mixed = jnp.concatenate([oa_ref[...], ob_ref[...], oc_ref[...], od_ref[...]], axis=1)
    x1 = x_ref[...] + _dot(mixed, wo_ref[...])
    out_ref[...] = x1
    ms = jnp.mean(x1 * x1, axis=-1, keepdims=True)
    h2 = (x1 * lax.rsqrt(ms + EPS) * g2_ref[...]).astype(BF16)

    @pl.when(it == 0)
    def _():
        hext[0:FFN_TAIL, :] = jnp.zeros((FFN_TAIL, D_MODEL), BF16)

    hext[FFN_TAIL:, :] = h2

    def conv(col0):
        u = _dot(hext[...], wup_ref[:, col0:col0 + FF_CHUNK])
        c = ctap_ref[:, col0:col0 + FF_CHUNK]
        r1 = pltpu.roll(u, 1, 0)
        r2 = pltpu.roll(u, 2, 0)
        y = c[0:1, :] * r2[FFN_TAIL:FFN_TAIL + tm] + c[1:2, :] * r1[FFN_TAIL:FFN_TAIL + tm]
        return y + c[2:3, :] * u[FFN_TAIL:FFN_TAIL + tm] + c[3:4, :]

    for j in range(D_FF // FF_CHUNK):
        gate = conv(j * FF_CHUNK)
        up = conv(D_FF + j * FF_CHUNK)
        act_scr[:, j * FF_CHUNK:(j + 1) * FF_CHUNK] = (_silu(gate) * up).astype(BF16)
    k1 = (D_FF // FF_CHUNK - FFN_LATE_CHUNKS) * FF_CHUNK
    out_ref[...] += _dot(act_scr[:, 0:k1], wd_ref[0:k1, :])
    out_ref[...] += _dot(act_scr[:, k1:D_FF], wd_ref[k1:D_FF, :])
    hext[0:FFN_TAIL, :] = hext[tm:tm + FFN_TAIL, :]


def _ffn(x2d, outs, wo, g2, wup, ctap, wd, b, s):
    tm = min(FFN_ROW_TILE, s)
    nt = s // tm
    kern = functools.partial(_ffn_kernel, tm=tm)
    row = lambda w: pl.BlockSpec((tm, w), lambda i, j: (i * nt + j, 0))
    return pl.pallas_call(
        kern,
        grid=(b, nt),
        in_specs=[row(D_MODEL)] + [row(GROUP_W)] * 4
                 + [_full_spec((D_MODEL, D_MODEL)), _full_spec((1, D_MODEL)),
                    _full_spec((D_MODEL, 2 * D_FF)), _full_spec((8, 2 * D_FF)), _full_spec((D_FF, D_MODEL))],
        out_specs=row(D_MODEL),
        out_shape=jax.ShapeDtypeStruct(x2d.shape, F32),
        scratch_shapes=[pltpu.VMEM((tm + FFN_TAIL, D_MODEL), BF16),
                        pltpu.VMEM((tm, D_FF), BF16)],
        compiler_params=_cparams(2),
        name="outproj_ffn",
    )(x2d, *outs, wo, g2, wup, ctap, wd)


def _constants():
    eye, strict, causal = _chunk_masks()
    return {
        "bd256": jnp.asarray(_block_mask(GROUP_W, HEAD_DIM, GROUP_W, HEAD_DIM), BF16),
        "bdk_gla": jnp.asarray(_block_mask(GROUP_W, HEAD_DIM, GLA_KW, GLA_DK), BF16),
        "expand_b": jnp.asarray(_expand_mat(GATE_GDN_B), BF16),
        "expand_g": jnp.asarray(_expand_mat(GATE_GDN_A), BF16),
        "chunk_masks": (jnp.asarray(eye), jnp.asarray(strict), jnp.asarray(causal)),
        "level_masks": jnp.asarray(_level_masks()),
        "ones64": jnp.ones((CHUNK, CHUNK), BF16),
        "ltri64": jnp.asarray(_tril_ones(CHUNK), BF16),
        "fox_routes": tuple(jnp.asarray(a, BF16) if a.ndim == 2 else jnp.asarray(a) for a in _fox_routes()),
    }


def _w_in_moves():
    sizes = (3 * GROUP_W, N_HEADS, 3 * GROUP_W, N_HEADS, N_HEADS, GROUP_W,
             GROUP_W, GROUP_W, GROUP_W, GROUP_W, 2 * GLA_KW, GROUP_W, GLA_RANK, GROUP_W)
    names = ("fox_qkv", "fox_f", "gdn_qkv", "gdn_b", "gdn_a", "gdn_z", "hg_q", "hg_f", "hg_i", "hg_g",
             "gla_qk", "gla_v", "gla_gk", "gla_g")
    src = dict(zip(names, np.concatenate([[0], np.cumsum(sizes)[:-1]])))
    wid = dict(zip(names, sizes))
    order = ("fox_qkv", "gdn_qkv", "gdn_z", "hg_q", "hg_f", "hg_i", "hg_g", "gla_qk", "gla_v", "gla_g",
             "fox_f", "gdn_b", "gdn_a", "gla_gk")
    moves, dst = [], 0
    for nm in order:
        moves.append((int(src[nm]), dst, wid[nm]))
        dst += wid[nm]
    return moves, dst


def _regroup_kernel(w_ref, o_ref):
    moves, used = _w_in_moves()
    x = w_ref[0]
    for src, dst, wd in moves:
        o_ref[0, :, dst:dst + wd] = x[:, src:src + wd].astype(BF16)
    o_ref[0, :, used:IN_COLS_PAD] = jnp.zeros((x.shape[0], IN_COLS_PAD - used), BF16)


def _regroup_w_in(w_in):
    depth, d, cols = w_in.shape
    tr = 256
    return pl.pallas_call(
        _regroup_kernel,
        grid=(depth, d // tr),
        in_specs=[pl.BlockSpec((1, tr, cols), lambda l, i: (l, i, 0))],
        out_specs=pl.BlockSpec((1, tr, IN_COLS_PAD), lambda l, i: (l, i, 0)),
        out_shape=jax.ShapeDtypeStruct((depth, d, IN_COLS_PAD), BF16),
        compiler_params=_cparams(2),
        name="regroup_w_in",
    )(w_in)


def _lane_row(vals, lane0, width):
    return jnp.zeros((1, width), F32).at[0, lane0:lane0 + vals.shape[0]].set(vals.astype(F32))


def _tile_heads(g):
    return jnp.tile(g.astype(F32), N_HEADS)[None, :]


def kernel(x, norm1_g, w_in, fox_qn_g, fox_kn_g, fox_b_f, fox_on_g, gdn_conv_w, gdn_a_log, gdn_dt_bias, gdn_on_g,
           hg_lb, hg_on_g, gla_w_gk, gla_b_gk, gla_on_g, w_out, norm2_g, w_up, ffn_conv_w, ffn_conv_b, w_down):
    b, s, d = x.shape
    depth = w_in.shape[0]
    consts = _constants()

    cs = jnp.cumsum(jax.nn.softmax(hg_lb.astype(F32), axis=0), axis=0)
    lower = cs - cs[0:1]

    w_in_all = _regroup_w_in(w_in)
    x2d = x.reshape(b * s, d)
    for l in range(depth):
        fox_qkv, gdn_qkv, gdn_z, hg, gla, gates = _inproj(x2d, norm1_g[l][None, :], w_in_all, l)

        qa, ka, va = _fox_prep(fox_qkv, gates, _lane_row(fox_b_f[l], GATE_FOX_F, GATE_W),
                               _tile_heads(fox_qn_g[l]), _tile_heads(fox_kn_g[l]), consts, b, s)
        o_a = _fox_attn(qa, ka, va, jnp.tile(fox_on_g[l].astype(F32), 2)[None, :], b, s)

        cw = jnp.zeros((8, 3 * GROUP_W), F32).at[0:GDN_CONV].set(gdn_conv_w[l])
        o_b = _gdn(gdn_qkv, gdn_z, gates, cw, _lane_row(gdn_a_log[l], GATE_GDN_A, GATE_W),
                   _lane_row(gdn_dt_bias[l], GATE_GDN_A, GATE_W), _tile_heads(gdn_on_g[l]), consts, b, s)

        lb = lower[l][None, :]
        o_c = _la("hgrn2", hg, (jnp.log(lb), jnp.log1p(-lb), 1.0 - lb), _tile_heads(hg_on_g[l]), consts, b, s)

        wgk = jnp.zeros((GATE_W, GLA_KW), F32).at[GATE_GLA_GK:GATE_GLA_GK + GLA_RANK].set(gla_w_gk[l]).astype(BF16)
        o_d = _la("gla", gla, (gates, wgk, gla_b_gk[l][None, :].astype(F32)), _tile_heads(gla_on_g[l]), consts, b, s)

        wo = w_out[l].astype(BF16)
        taps = jnp.concatenate([ffn_conv_w[l], ffn_conv_b[l][None, :],
                                jnp.zeros((8 - FFN_CONV - 1, 2 * D_FF), F32)], axis=0)
        x2d = _ffn(x2d, (o_a, o_b, o_c, o_d), wo, norm2_g[l][None, :], w_up[l].astype(BF16), taps,
                   w_down[l].astype(BF16), b, s)
    return x2d.reshape(b, s, d)
```

```python
import functools

import numpy as np
import jax
import jax.numpy as jnp
from jax import lax
from jax.experimental import pallas as pl
from jax.experimental.pallas import tpu as pltpu

F32 = jnp.float32
BF16 = jnp.bfloat16

D_MODEL = 1024
N_HEADS = 4
HEAD_DIM = 64
GROUP_W = N_HEADS * HEAD_DIM
GLA_DK = 32
GLA_KW = N_HEADS * GLA_DK
GLA_RANK = 16
GLA_NORM = 16.0
GDN_CONV = 4
D_FF = 2816
FFN_CONV = 3
EPS = 1e-6
CHUNK = 64
WY_GROUP = 4
LA_GROUP_ROWS = {"hgrn2": 256, "gla": 512}

LANES = 128
GATE_W = LANES
GATE_FOX_F, GATE_GDN_B, GATE_GDN_A, GATE_GLA_GK = 0, 4, 8, 12
assert GATE_FOX_F == 0

IN_SEGS = (3 * GROUP_W, 3 * GROUP_W, GROUP_W, 4 * GROUP_W, 3 * GROUP_W, GATE_W)
IN_COLS_PAD = sum(IN_SEGS)

ROW_TILE = 512
FFN_ROW_TILE = 512
FF_CHUNK = 256
FOX_BLOCK = 512
NEG_BIG = -1e30
LOG2_E = 1.4426950408889634

VMEM_LIMIT = 56 * 1024 * 1024


def _dot(a, b):
    return jnp.dot(a, b, preferred_element_type=F32)


def _dot_nt(a, b):
    return lax.dot_general(a, b, (((1,), (1,)), ((), ())), preferred_element_type=F32)


def _dot_tn(a, b):
    return lax.dot_general(a, b, (((0,), (0,)), ((), ())), preferred_element_type=F32)


def _split3(x):
    hi = x.astype(BF16)
    r1 = x - hi.astype(F32)
    mid = r1.astype(BF16)
    lo = (r1 - mid.astype(F32)).astype(BF16)
    return hi, mid, lo


def _dot01_l(m01, x):
    hi, mid, lo = _split3(x)
    return _dot(m01, hi) + _dot(m01, mid) + _dot(m01, lo)


def _dot01_r(x, m01):
    hi, mid, lo = _split3(x)
    return _dot(hi, m01) + _dot(mid, m01) + _dot(lo, m01)


def _seg_meansq(x, bd16, width):
    sq = x * x
    hi = sq.astype(BF16)
    lo = (sq - hi.astype(F32)).astype(BF16)
    return (_dot(hi, bd16) + _dot(lo, bd16)) * (1.0 / width)


def _log_sigmoid(x):
    return jnp.minimum(x, 0.0) - jnp.log1p(jnp.exp(-jnp.abs(x)))


def _softplus(x):
    return jnp.maximum(x, 0.0) + jnp.log1p(jnp.exp(-jnp.abs(x)))


def _silu(x):
    return x * jax.nn.sigmoid(x)


def _tile_rows(x, n):
    return jnp.concatenate([x] * n, axis=0)


def _interleave(*gens):
    live = list(gens)
    while live:
        for gen in list(live):
            try:
                next(gen)
            except StopIteration:
                live.remove(gen)


def _software_pipeline(phases, n_steps):
    def step(t):
        _interleave(*[ph(t - d) for d, ph in enumerate(phases) if isinstance(t, jax.Array) or 0 <= t - d < n_steps])

    depth = len(phases) - 1
    for t in range(n_steps + depth):
        if depth <= t < n_steps:
            if t == depth:
                lax.fori_loop(depth, n_steps, lambda tt, c: (step(tt), c)[1], 0)
        else:
            step(t)


def _full_spec(shape):
    nd = len(shape)
    return pl.BlockSpec(shape, lambda *_: (0,) * nd, pipeline_mode=pl.Buffered(1))


def _cparams(n_axes, flags=None):
    return pltpu.CompilerParams(dimension_semantics=("arbitrary",) * n_axes, vmem_limit_bytes=VMEM_LIMIT,
                                flags=flags)


def _block_mask(rows, row_blk, cols, col_blk):
    r = np.arange(rows)[:, None] // row_blk
    c = np.arange(cols)[None, :] // col_blk
    return (r == c).astype(np.float32)


def _tril_ones(n):
    return np.tril(np.ones((n, n), np.float32))


def _expand_mat(lane0):
    m = np.zeros((GATE_W, GROUP_W), np.float32)
    for h in range(N_HEADS):
        m[lane0 + h, h * HEAD_DIM:(h + 1) * HEAD_DIM] = 1.0
    return m


def _chunk_masks():
    r = np.arange(CHUNK)[:, None]
    s = np.arange(GROUP_W)[None, :] % CHUNK
    return ((r == s).astype(np.float32), (r > s).astype(np.float32), (r >= s).astype(np.float32))


LEVELS = (1, 2, 4, 8, 16, 32)


def _level_masks():
    r = np.arange(CHUNK)[:, None]
    s = np.arange(GROUP_W)[None, :] % CHUNK
    out = []
    for m in LEVELS:
        same = (r // (2 * m)) == (s // (2 * m))
        out.append((same & ((r % (2 * m)) >= m) & ((s % (2 * m)) < m)).astype(np.float32))
    return np.stack(out)


def _fox_routes():
    rq = np.zeros((GATE_W, N_HEADS * LANES), np.float32)
    rk = np.zeros((GATE_W, N_HEADS * LANES), np.float32)
    oq = np.zeros((N_HEADS, 1, LANES), np.float32)
    ok = np.zeros((N_HEADS, 1, LANES), np.float32)
    for h in range(N_HEADS):
        base = (1 - h % 2) * HEAD_DIM
        for j in range(3):
            rq[j * N_HEADS + h, h * LANES + base + j] = 1.0
            rk[j * N_HEADS + h, h * LANES + base + 3 + j] = -1.0
            oq[h, 0, base + 3 + j] = 1.0
            ok[h, 0, base + j] = 1.0
    return rq, rk, oq, ok


def _inproj_kernel(x_ref, g_ref, w_ref, *out_refs):
    x = x_ref[...]
    ms = jnp.mean(x * x, axis=-1, keepdims=True)
    h = (x * lax.rsqrt(ms + EPS) * g_ref[...]).astype(BF16)
    off = 0
    for ref, width in zip(out_refs, IN_SEGS):
        step = min(width, 256)
        for c in range(0, width, step):
            ref[:, c:c + step] = _dot(h, w_ref[:, off + c:off + c + step])
        off += width


def _inproj(x2d, g, w_all, layer):
    t = x2d.shape[0]
    tm = min(ROW_TILE, t)
    return pl.pallas_call(
        _inproj_kernel,
        grid=(t // tm,),
        in_specs=[pl.BlockSpec((tm, D_MODEL), lambda i: (i, 0)),
                  _full_spec((1, D_MODEL)),
                  pl.BlockSpec((None, D_MODEL, IN_COLS_PAD), lambda i: (layer, 0, 0), pipeline_mode=pl.Buffered(1))],
        out_specs=[pl.BlockSpec((tm, wd), lambda i: (i, 0)) for wd in IN_SEGS],
        out_shape=[jax.ShapeDtypeStruct((t, wd), F32) for wd in IN_SEGS],
        compiler_params=_cparams(1),
        name="inproj",
    )(x2d, g, w_all)


def _fox_prep_kernel(qkv_ref, gates_ref, bf_ref, qg_ref, kg_ref, bd_ref, ltri_ref, rq_ref, rk_ref, oq_ref, ok_ref,
                     qa_ref, ka_ref, va_ref, *, rb):
    s_len = qkv_ref.shape[0]
    lane = lax.broadcasted_iota(jnp.int32, (rb, LANES), 1)

    def blk(i, carry):
        r0 = pl.multiple_of(i * rb, rb)
        logf = _log_sigmoid(gates_ref[pl.ds(r0, rb), :] + bf_ref[...])
        c = _dot01_l(ltri_ref[...], logf) + carry
        hi, mid, lo = [t.astype(F32) for t in _split3(c)]
        c3 = jnp.where(lane < N_HEADS, hi, jnp.where(lane < 2 * N_HEADS, pltpu.roll(mid, N_HEADS, 1),
                                                       pltpu.roll(lo, 2 * N_HEADS, 1))).astype(BF16)
        qkv = qkv_ref[pl.ds(r0, rb), :]
        q = qkv[:, 0:GROUP_W]
        k = qkv[:, GROUP_W:2 * GROUP_W]
        v = qkv[:, 2 * GROUP_W:3 * GROUP_W]
        bd = bd_ref[...]
        qn = q * lax.rsqrt(_seg_meansq(q, bd, HEAD_DIM) + EPS) * qg_ref[...] * (HEAD_DIM ** -0.5)
        kn = k * lax.rsqrt(_seg_meansq(k, bd, HEAD_DIM) + EPS) * kg_ref[...]
        q_route = _dot(c3, rq_ref[...])
        k_route = _dot(c3, rk_ref[...])
        for h in range(N_HEADS):
            p = h // 2
            own = (lane // HEAD_DIM) == (h % 2)
            qa = jnp.where(own, qn[:, p * LANES:(p + 1) * LANES], q_route[:, h * LANES:(h + 1) * LANES] + oq_ref[h])
            ka = jnp.where(own, kn[:, p * LANES:(p + 1) * LANES], k_route[:, h * LANES:(h + 1) * LANES] + ok_ref[h])
            va = jnp.where(own, v[:, p * LANES:(p + 1) * LANES], jnp.where(lane == (1 - h % 2) * HEAD_DIM, 1.0, 0.0))
            qa_ref[0, h, pl.ds(r0, rb), :] = qa.astype(BF16)
            ka_ref[0, h, pl.ds(r0, rb), :] = ka.astype(BF16)
            va_ref[0, h, pl.ds(r0, rb), :] = va.astype(BF16)
        return c[rb - 1:rb, :]

    lax.fori_loop(0, s_len // rb, blk, jnp.zeros((1, GATE_W), F32))


def _fox_prep(fox_qkv, gates, bf_row, qg_row, kg_row, consts, b, s):
    rb = min(256, s)
    rq, rk, oq, ok = consts["fox_routes"]
    ltri = jnp.asarray(_tril_ones(rb), BF16)
    kern = functools.partial(_fox_prep_kernel, rb=rb)
    return pl.pallas_call(
        kern,
        grid=(b,),
        in_specs=[pl.BlockSpec((s, 3 * GROUP_W), lambda i: (i, 0)),
                  pl.BlockSpec((s, GATE_W), lambda i: (i, 0)),
                  _full_spec((1, GATE_W)), _full_spec((1, GROUP_W)), _full_spec((1, GROUP_W)),
                  _full_spec((GROUP_W, GROUP_W)), _full_spec((rb, rb)),
                  _full_spec(rq.shape), _full_spec(rk.shape), _full_spec(oq.shape), _full_spec(ok.shape)],
        out_specs=[pl.BlockSpec((1, N_HEADS, s, LANES), lambda i: (i, 0, 0, 0)),
                   pl.BlockSpec((1, N_HEADS, s, LANES), lambda i: (i, 0, 0, 0)),
                   pl.BlockSpec((1, N_HEADS, s, LANES), lambda i: (i, 0, 0, 0))],
        out_shape=[jax.ShapeDtypeStruct((b, N_HEADS, s, LANES), BF16)] * 3,
        compiler_params=_cparams(1),
        name="fox_prep",
    )(fox_qkv, gates, bf_row, qg_row, kg_row, consts["bd256"], ltri, rq, rk, oq, ok)


def _fox_attn_kernel(q_ref, k_ref, v_ref, og_ref, o_ref, s_scr, m_scr, acc_scr, *, blk, nq):
    lane = lax.broadcasted_iota(jnp.int32, (blk, LANES), 1)
    row = lax.broadcasted_iota(jnp.int32, (blk, blk), 0)
    col = lax.broadcasted_iota(jnp.int32, (blk, blk), 1)

    def fold(s):
        m = s[:, 0:LANES]
        for c in range(1, blk // LANES):
            m = jnp.maximum(m, s[:, c * LANES:(c + 1) * LANES])
        return m

    def logit_stages(i):
        slot = i % 2
        for j in range(i + 1):
            for hh in range(2):
                s = _dot_nt(q_ref[0, hh, i * blk:(i + 1) * blk, :], k_ref[0, hh, j * blk:(j + 1) * blk, :])
                if j == i:
                    s = jnp.where(col <= row, s, NEG_BIG)
                s_scr[slot, hh, j] = s
                m_scr[slot, hh] = fold(s) if j == 0 else jnp.maximum(m_scr[slot, hh], fold(s))
            yield

    def value_stages(i):
        slot = i % 2
        row_max = [jnp.max(m_scr[slot, hh], axis=-1, keepdims=True) for hh in range(2)]
        for j in range(i + 1):
            for hh in range(2):
                p = jnp.exp(s_scr[slot, hh, j] - row_max[hh])
                pv = _dot(p.astype(BF16), v_ref[0, hh, j * blk:(j + 1) * blk, :])
                acc_scr[slot, hh] = pv if j == 0 else acc_scr[slot, hh] + pv
            yield
        outs = []
        for hh in range(2):
            acc = acc_scr[slot, hh]
            l = jnp.sum(jnp.where(lane == (1 - hh) * HEAD_DIM, acc, 0.0), axis=-1, keepdims=True)
            o = acc / l
            own = (lane // HEAD_DIM) == hh
            ms = jnp.sum(jnp.where(own, o * o, 0.0), axis=-1, keepdims=True) * (1.0 / HEAD_DIM)
            outs.append(o * lax.rsqrt(ms + EPS) * og_ref[...])
        o_ref[i * blk:(i + 1) * blk, :] = jnp.where((lane // HEAD_DIM) == 0, outs[0], outs[1]).astype(BF16)

    _interleave(logit_stages(0))
    for i in range(nq):
        if i + 1 < nq:
            _interleave(value_stages(i), logit_stages(i + 1))
        else:
            _interleave(value_stages(i))


def _fox_attn(qa, ka, va, og_row, b, s):
    blk = min(FOX_BLOCK, s)
    nq = s // blk
    kern = functools.partial(_fox_attn_kernel, blk=blk, nq=nq)
    head_pair = pl.BlockSpec((1, 2, s, LANES), lambda i, p: (i, p, 0, 0))
    return pl.pallas_call(
        kern,
        grid=(b, 2),
        in_specs=[head_pair, head_pair, head_pair, _full_spec((1, LANES))],
        out_specs=pl.BlockSpec((s, LANES), lambda i, p: (i, p)),
        out_shape=jax.ShapeDtypeStruct((b * s, GROUP_W), BF16),
        scratch_shapes=[pltpu.VMEM((2, 2, nq, blk, blk), F32),
                        pltpu.VMEM((2, 2, blk, LANES), F32),
                        pltpu.VMEM((2, 2, blk, LANES), F32)],
        compiler_params=_cparams(2),
        name="fox_attn",
    )(qa, ka, va, og_row)


def _gdn_kernel(qkv_ref, z_ref, gates_ref, cw_ref, alog_ref, dt_ref, og_ref, bd_ref, eb_ref, eg_ref,
                eye_ref, strict_ref, causal_ref, ones_ref, ctri_ref,
                o_ref, xpad, q_scr, k_scr, v_scr, beta_scr, g_scr, o_scr, u_scr, s_scr,
                wq_scr, aqk_scr, kout_scr, *, rb):
    s_len = qkv_ref.shape[0]
    nblk = s_len // rb
    bd16 = bd_ref[...]
    bdf = bd16.astype(F32)

    xpad[0:8, :] = jnp.zeros((8, 3 * GROUP_W), F32)
    xpad[8:, :] = qkv_ref[...]

    def prep_stages(i):
        r0 = pl.multiple_of(i * rb, rb)

        gt = gates_ref[pl.ds(r0, rb), :]
        beta_scr[pl.ds(r0, rb), :] = _dot01_r(jax.nn.sigmoid(gt), eb_ref[...])
        g_s = -jnp.exp(alog_ref[...]) * _softplus(gt + dt_ref[...])
        g_rep = [_dot(t, eg_ref[...]).astype(BF16) for t in _split3(g_s)]

        def conv_silu(c0):
            xx = xpad[pl.ds(r0, rb + 8), c0:c0 + LANES]
            y = cw_ref[GDN_CONV - 1:GDN_CONV, c0:c0 + LANES] * xx[8:8 + rb]
            for j in range(GDN_CONV - 1):
                y = y + cw_ref[j:j + 1, c0:c0 + LANES] * pltpu.roll(xx, GDN_CONV - 1 - j, 0)[8:8 + rb]
            return _silu(y)

        yield
        q_lo = conv_silu(0)
        yield
        q = jnp.concatenate([q_lo, conv_silu(LANES)], axis=1)
        q_ss = _seg_meansq(q, bd16, 1.0)
        yield
        k_lo = conv_silu(2 * LANES)
        yield
        k = jnp.concatenate([k_lo, conv_silu(3 * LANES)], axis=1)
        k_ss = _seg_meansq(k, bd16, 1.0)
        yield
        v_scr[pl.ds(r0, rb), 0:LANES] = conv_silu(4 * LANES)
        yield
        v_scr[pl.ds(r0, rb), LANES:2 * LANES] = conv_silu(5 * LANES)
        ct = ctri_ref[...]
        g_scr[pl.ds(r0, rb), :] = _dot(ct, g_rep[0]) + _dot(ct, g_rep[1]) + _dot(ct, g_rep[2])
        yield
        q_scr[pl.ds(r0, rb), :] = q * lax.rsqrt(q_ss + EPS) * (HEAD_DIM ** -0.5)
        k_scr[pl.ds(r0, rb), :] = k * lax.rsqrt(k_ss + EPS)

    def bd(y16):
        return _tile_rows(y16, N_HEADS) * bd16

    eye = eye_ref[...]

    def wy_stages(i):
        ns = [i * WY_GROUP + c for c in range(WY_GROUP)]
        r0s = [pl.multiple_of(n * CHUNK, CHUNK) for n in ns]
        q = [q_scr[pl.ds(r0, CHUNK), :] for r0 in r0s]
        k = [k_scr[pl.ds(r0, CHUNK), :] for r0 in r0s]
        beta = [beta_scr[pl.ds(r0, CHUNK), :] for r0 in r0s]
        g = [g_scr[pl.ds(r0, CHUNK), :] for r0 in r0s]
        kb = [kc * bc for kc, bc in zip(k, beta)]
        aa = [_dot_nt(jnp.concatenate([kbc.astype(BF16), qc.astype(BF16)], axis=0), bd(kc.astype(BF16)))
              for kbc, qc, kc in zip(kb, q, k)]
        g_row = [_dot01_l(ones_ref[...], gc * eye) for gc in g]
        yield
        decay = [jnp.exp(jnp.minimum(gc - grc, 0.0)) for gc, grc in zip(g, g_row)]
        m = [ac[0:CHUNK] * dc * strict_ref[...] for ac, dc in zip(aa, decay)]
        for r0, ac, dc in zip(r0s, aa, decay):
            aqk_scr[pl.ds(r0, CHUNK), :] = (ac[CHUNK:2 * CHUNK] * dc * causal_ref[...]).astype(BF16)
        pm = [eye - mc for mc in m]
        qm = [_dot(mc.astype(BF16), bd(mc.astype(BF16))) for mc in m]
        yield
        for it in range(5):
            q16 = [qc.astype(BF16) for qc in qm]
            if it < 4:
                r = [_dot(jnp.concatenate([pc.astype(BF16), qc], axis=0), bd(qc)) for pc, qc in zip(pm, q16)]
                pm = [pc + rc[0:CHUNK] for pc, rc in zip(pm, r)]
                qm = [rc[CHUNK:2 * CHUNK] for rc in r]
            else:
                pm = [pc + _dot(pc.astype(BF16), bd(qc)) for pc, qc in zip(pm, q16)]
            yield
        t16 = [pc.astype(BF16) for pc in pm]
        eg = [jnp.exp(gc) for gc in g]
        for c, (n, r0) in enumerate(zip(ns, r0s)):
            v = v_scr[pl.ds(r0, CHUNK), :]
            g_last = g_scr[pl.ds(r0 + CHUNK - 1, 1), :]
            wq_scr[n, 0:CHUNK, :] = _dot(t16[c], bd((kb[c] * eg[c]).astype(BF16))).astype(BF16)
            wq_scr[n, CHUNK:2 * CHUNK, :] = (q[c] * eg[c]).astype(BF16)
            u_scr[pl.ds(r0, CHUNK), :] = _dot(t16[c], bd((v * beta[c]).astype(BF16)))
            kout_scr[pl.ds(r0, CHUNK), :] = (k[c] * jnp.exp(g_last - g[c])).astype(BF16)

    def scan_stages(i):
        st = s_scr[...]
        for c in range(WY_GROUP):
            n = i * WY_GROUP + c
            r0 = pl.multiple_of(n * CHUNK, CHUNK)
            wq = _dot(wq_scr[n], st.astype(BF16))
            yield
            v16 = (u_scr[pl.ds(r0, CHUNK), :] - wq[0:CHUNK]).astype(BF16)
            a_last = jnp.exp(g_scr[pl.ds(r0 + CHUNK - 1, 1), :])
            st = st * a_last + _dot_tn(kout_scr[pl.ds(r0, CHUNK), :], v16) * bdf
            o_scr[pl.ds(r0, CHUNK), :] = wq[CHUNK:2 * CHUNK] + _dot(aqk_scr[pl.ds(r0, CHUNK), :], bd(v16))
            yield
        s_scr[...] = st

    def out_stages(i):
        r0 = pl.multiple_of(i * rb, rb)
        o = o_scr[pl.ds(r0, rb), :]
        o_ms = _seg_meansq(o, bd16, HEAD_DIM)
        yield
        on = o * lax.rsqrt(o_ms + EPS) * og_ref[...]
        o_ref[pl.ds(r0, rb), :] = (on * _silu(z_ref[pl.ds(r0, rb), :])).astype(BF16)

    s_scr[...] = jnp.zeros((GROUP_W, GROUP_W), F32)
    _software_pipeline((prep_stages, wy_stages, scan_stages, out_stages), nblk)


def _gdn(gdn_qkv, gdn_z, gates, cw, alog_row, dt_row, og_row, consts, b, s):
    rb = min(256, s)
    eye, strict, causal = consts["chunk_masks"]
    kern = functools.partial(_gdn_kernel, rb=rb)
    row = lambda w: pl.BlockSpec((s, w), lambda i: (i, 0))
    return pl.pallas_call(
        kern,
        grid=(b,),
        in_specs=[row(3 * GROUP_W), row(GROUP_W), row(GATE_W),
                  _full_spec((8, 3 * GROUP_W)), _full_spec((1, GATE_W)), _full_spec((1, GATE_W)),
                  _full_spec((1, GROUP_W)), _full_spec((GROUP_W, GROUP_W)),
                  _full_spec((GATE_W, GROUP_W)), _full_spec((GATE_W, GROUP_W)),
                  _full_spec((CHUNK, GROUP_W)), _full_spec((CHUNK, GROUP_W)), _full_spec((CHUNK, GROUP_W)),
                  _full_spec((CHUNK, CHUNK)), _full_spec((rb, rb))],
        out_specs=row(GROUP_W),
        out_shape=jax.ShapeDtypeStruct((b * s, GROUP_W), BF16),
        scratch_shapes=[pltpu.VMEM((s + 8, 3 * GROUP_W), F32)]
                       + [pltpu.VMEM((s, GROUP_W), F32) for _ in range(7)]
                       + [pltpu.VMEM((GROUP_W, GROUP_W), F32),
                          pltpu.VMEM((s // CHUNK, 2 * CHUNK, GROUP_W), BF16),
                          pltpu.VMEM((s, GROUP_W), BF16), pltpu.VMEM((s, GROUP_W), BF16)],
        compiler_params=_cparams(1),
        name="gdn",
    )(gdn_qkv, gdn_z, gates, cw, alog_row, dt_row, og_row, consts["bd256"], consts["expand_b"], consts["expand_g"],
      eye, strict, causal, consts["ones64"],
      jnp.asarray(_tril_ones(rb) * _block_mask(rb, CHUNK, rb, CHUNK), BF16))


def _la_kernel(*refs, variant, rb):
    if variant == "hgrn2":
        (x_ref, la_ref, l1_ref, oml_ref, og_ref, bdv_ref, bdk_ref, eye_ref, lvl_ref, ltri_ref,
         o_ref, q_scr, k_scr, a_scr, v_scr, gate_scr, gc_scr, o_scr, st_scr) = refs
        kw, dk = GROUP_W, HEAD_DIM
    else:
        (x_ref, gates_ref, wgk_ref, bgk_ref, og_ref, bdv_ref, bdk_ref, eye_ref, lvl_ref, ltri_ref,
         o_ref, q_scr, k_scr, a_scr, v_scr, gate_scr, gc_scr, o_scr, st_scr) = refs
        kw, dk = GLA_KW, GLA_DK
    s_len = x_ref.shape[0]
    nblk = s_len // rb
    bdv16 = bdv_ref[...]
    bdk16 = bdk_ref[...]
    bdkf = bdk16.astype(F32)

    def prep_stages(i):
        r0 = pl.multiple_of(i * rb, rb)
        if variant == "hgrn2":
            f_logit = x_ref[pl.ds(r0, rb), GROUP_W:2 * GROUP_W]
            q_scr[pl.ds(r0, rb), :] = _silu(x_ref[pl.ds(r0, rb), 0:GROUP_W]) * (dk ** -0.5)
            yield
            a = la_ref[...]
            bterm = l1_ref[...] + _log_sigmoid(f_logit)
            amax = jnp.maximum(a, bterm)
            a_scr[pl.ds(r0, rb), :] = amax + jnp.log1p(jnp.exp(-jnp.abs(a - bterm)))
            yield
            k_scr[pl.ds(r0, rb), :] = oml_ref[...] * jax.nn.sigmoid(-f_logit)
            yield
            v_scr[pl.ds(r0, rb), :] = x_ref[pl.ds(r0, rb), 2 * GROUP_W:3 * GROUP_W]
            gate_scr[pl.ds(r0, rb), :] = x_ref[pl.ds(r0, rb), 3 * GROUP_W:4 * GROUP_W]
        else:
            lr = _dot(gates_ref[pl.ds(r0, rb), :].astype(BF16), wgk_ref[...]) + bgk_ref[...]
            q_scr[pl.ds(r0, rb), :] = x_ref[pl.ds(r0, rb), 0:kw] * (dk ** -0.5)
            k_scr[pl.ds(r0, rb), :] = x_ref[pl.ds(r0, rb), kw:2 * kw]
            yield
            v_scr[pl.ds(r0, rb), :] = x_ref[pl.ds(r0, rb), 2 * kw:2 * kw + GROUP_W]
            gate_scr[pl.ds(r0, rb), :] = x_ref[pl.ds(r0, rb), 2 * kw + GROUP_W:2 * kw + 2 * GROUP_W]
            yield
            a_scr[pl.ds(r0, rb), :] = _log_sigmoid(lr) * (1.0 / GLA_NORM)

    sub = lax.broadcasted_iota(jnp.int32, (8, kw), 0)

    def level_ref(m, slot):
        def brow(r):
            return jnp.broadcast_to(gc_scr[slot, r:r + 1, :], (8, kw))
        pieces = []
        for a in range(CHUNK // 8):
            if 2 * m >= 8:
                pieces.append(brow((8 * a) // (2 * m) * (2 * m) + m - 1))
            elif m == 2:
                pieces.append(jnp.where(sub < 4, brow(8 * a + 1), brow(8 * a + 5)))
            else:
                p = jnp.where(sub < 2, brow(8 * a), brow(8 * a + 2))
                p = jnp.where(sub < 4, p, jnp.where(sub < 6, brow(8 * a + 4), brow(8 * a + 6)))
                pieces.append(p)
        return jnp.concatenate(pieces, axis=0)

    def chunk_stages(i):
        slots = range(rb // CHUNK)
        r0s = [pl.multiple_of(i * rb + c * CHUNK, CHUNK) for c in slots]
        q = [q_scr[pl.ds(r0, CHUNK), :] for r0 in r0s]
        k = [k_scr[pl.ds(r0, CHUNK), :] for r0 in r0s]
        v16 = [v_scr[pl.ds(r0, CHUNK), :].astype(BF16) for r0 in r0s]
        g = [_dot01_l(ltri_ref[...], a_scr[pl.ds(r0, CHUNK), :]) * LOG2_E for r0 in r0s]
        yield
        for c in slots:
            gc_scr[c] = g[c]
        q16 = [qc.astype(BF16) for qc in q]
        k16 = [kc.astype(BF16) for kc in k]
        a_in = [_dot_nt(q16[c], _tile_rows(k16[c], N_HEADS) * bdk16) * eye_ref[...] for c in slots]
        yield
        for li, m in enumerate(LEVELS):
            for c in slots:
                e = jnp.exp2(-jnp.abs(g[c] - level_ref(m, c))).astype(BF16)
                a_in[c] = a_in[c] + _dot_nt(q16[c] * e, _tile_rows(k16[c] * e, N_HEADS) * bdk16) * lvl_ref[li]
            yield
        o = [_dot(a_in[c].astype(BF16), _tile_rows(v16[c], N_HEADS) * bdv16) for c in slots]
        ds = [_dot_tn(v16[c], (k[c] * jnp.exp2(g[c][CHUNK - 1:CHUNK, :] - g[c])).astype(BF16)) * bdkf for c in slots]
        q_in = [(q[c] * jnp.exp2(g[c])).astype(BF16) for c in slots]
        yield
        st = st_scr[...]
        for c in slots:
            o_scr[pl.ds(r0s[c], CHUNK), :] = o[c] + _dot_nt(q_in[c], st.astype(BF16))
            st = st * jnp.exp2(g[c][CHUNK - 1:CHUNK, :]) + ds[c]
        st_scr[...] = st

    def out_stages(i):
        r0 = pl.multiple_of(i * rb, rb)
        o = o_scr[pl.ds(r0, rb), :]
        o_ms = _seg_meansq(o, bdv16, HEAD_DIM)
        yield
        on = o * lax.rsqrt(o_ms + EPS) * og_ref[...]
        o_ref[pl.ds(r0, rb), :] = (on * _silu(gate_scr[pl.ds(r0, rb), :])).astype(BF16)

    st_scr[...] = jnp.zeros((GROUP_W, kw), F32)
    _software_pipeline((prep_stages, chunk_stages, out_stages), nblk)


def _la(variant, x, extra, og_row, consts, b, s):
    rb = min(LA_GROUP_ROWS[variant], s)
    kw = GROUP_W if variant == "hgrn2" else GLA_KW
    xw = 4 * GROUP_W if variant == "hgrn2" else 3 * GROUP_W
    kern = functools.partial(_la_kernel, variant=variant, rb=rb)
    row = lambda w: pl.BlockSpec((s, w), lambda i: (i, 0))
    if variant == "hgrn2":
        in_specs = [row(xw)] + [_full_spec((1, GROUP_W))] * 3
        bdk = consts["bd256"]
    else:
        in_specs = [row(xw), row(GATE_W), _full_spec((GATE_W, GLA_KW)), _full_spec((1, GLA_KW))]
        bdk = consts["bdk_gla"]
    in_specs += [_full_spec((1, GROUP_W)), _full_spec((GROUP_W, GROUP_W)), _full_spec((GROUP_W, kw)),
                 _full_spec((CHUNK, GROUP_W)), _full_spec((len(LEVELS), CHUNK, GROUP_W)), _full_spec((CHUNK, CHUNK))]
    return pl.pallas_call(
        kern,
        grid=(b,),
        in_specs=in_specs,
        out_specs=row(GROUP_W),
        out_shape=jax.ShapeDtypeStruct((b * s, GROUP_W), BF16),
        scratch_shapes=[pltpu.VMEM((s, kw), F32) for _ in range(3)]
                       + [pltpu.VMEM((s, GROUP_W), F32) for _ in range(2)]
                       + [pltpu.VMEM((rb // CHUNK, CHUNK, kw), F32), pltpu.VMEM((s, GROUP_W), F32),
                          pltpu.VMEM((GROUP_W, kw), F32)],
        compiler_params=_cparams(1),
        name="la_" + variant,
    )(x, *extra, og_row, consts["bd256"], bdk, consts["chunk_masks"][0], consts["level_masks"], consts["ltri64"])


FFN_TAIL = 16
FFN_LATE_CHUNKS = 2


def _ffn_kernel(x_ref, oa_ref, ob_ref, oc_ref, od_ref, wo_ref, g2_ref, wup_ref, ctap_ref, wd_ref,
                out_ref, hext, act_scr, *, tm):
    it = pl.program_id(1)
    mixed = jnp.concatenate([oa_ref[...], ob_ref[...], oc_ref[...], od_ref[...]], axis=1)
    x1 = x_ref[...] + _dot(mixed, wo_ref[...])
    out_ref[...] = x1
    ms = jnp.mean(x1 * x1, axis=-1, keepdims=True)
    h2 = (x1 * lax.rsqrt(ms + EPS) * g2_ref[...]).astype(BF16)

    @pl.when(it == 0)
    def _():
        hext[0:FFN_TAIL, :] = jnp.zeros((FFN_TAIL, D_MODEL), BF16)

    hext[FFN_TAIL:, :] = h2

    def conv(col0):
        u = _dot(hext[...], wup_ref[:, col0:col0 + FF_CHUNK])
        c = ctap_ref[:, col0:col0 + FF_CHUNK]
        r1 = pltpu.roll(u, 1, 0)
        r2 = pltpu.roll(u, 2, 0)
        y = c[0:1, :] * r2[FFN_TAIL:FFN_TAIL + tm] + c[1:2, :] * r1[FFN_TAIL:FFN_TAIL + tm]
        return y + c[2:3, :] * u[FFN_TAIL:FFN_TAIL + tm] + c[3:4, :]

    for j in range(D_FF // FF_CHUNK):
        gate = conv(j * FF_CHUNK)
        up = conv(D_FF + j * FF_CHUNK)
        act_scr[:, j * FF_CHUNK:(j + 1) * FF_CHUNK] = (_silu(gate) * up).astype(BF16)
    k1 = (D_FF // FF_CHUNK - FFN_LATE_CHUNKS) * FF_CHUNK
    out_ref[...] += _dot(act_scr[:, 0:k1], wd_ref[0:k1, :])
    out_ref[...] += _dot(act_scr[:, k1:D_FF], wd_ref[k1:D_FF, :])
    hext[0:FFN_TAIL, :] = hext[tm:tm + FFN_TAIL, :]


def _ffn(x2d, outs, wo, g2, wup, ctap, wd, b, s):
    tm = min(FFN_ROW_TILE, s)
    nt = s // tm
    kern = functools.partial(_ffn_kernel, tm=tm)
    row = lambda w: pl.BlockSpec((tm, w), lambda i, j: (i * nt + j, 0))
    return pl.pallas_call(
        kern,
        grid=(b, nt),
        in_specs=[row(D_MODEL)] + [row(GROUP_W)] * 4
                 + [_full_spec((D_MODEL, D_MODEL)), _full_spec((1, D_MODEL)),
                    _full_spec((D_MODEL, 2 * D_FF)), _full_spec((8, 2 * D_FF)), _full_spec((D_FF, D_MODEL))],
        out_specs=row(D_MODEL),
        out_shape=jax.ShapeDtypeStruct(x2d.shape, F32),
        scratch_shapes=[pltpu.VMEM((tm + FFN_TAIL, D_MODEL), BF16),
                        pltpu.VMEM((tm, D_FF), BF16)],
        compiler_params=_cparams(2),
        name="outproj_ffn",
    )(x2d, *outs, wo, g2, wup, ctap, wd)


def _constants():
    eye, strict, causal = _chunk_masks()
    return {
        "bd256": jnp.asarray(_block_mask(GROUP_W, HEAD_DIM, GROUP_W, HEAD_DIM), BF16),
        "bdk_gla": jnp.asarray(_block_mask(GROUP_W, HEAD_DIM, GLA_KW, GLA_DK), BF16),
        "expand_b": jnp.asarray(_expand_mat(GATE_GDN_B), BF16),
        "expand_g": jnp.asarray(_expand_mat(GATE_GDN_A), BF16),
        "chunk_masks": (jnp.asarray(eye), jnp.asarray(strict), jnp.asarray(causal)),
        "level_masks": jnp.asarray(_level_masks()),
        "ones64": jnp.ones((CHUNK, CHUNK), BF16),
        "ltri64": jnp.asarray(_tril_ones(CHUNK), BF16),
        "fox_routes": tuple(jnp.asarray(a, BF16) if a.ndim == 2 else jnp.asarray(a) for a in _fox_routes()),
    }


def _w_in_moves():
    sizes = (3 * GROUP_W, N_HEADS, 3 * GROUP_W, N_HEADS, N_HEADS, GROUP_W,
             GROUP_W, GROUP_W, GROUP_W, GROUP_W, 2 * GLA_KW, GROUP_W, GLA_RANK, GROUP_W)
    names = ("fox_qkv", "fox_f", "gdn_qkv", "gdn_b", "gdn_a", "gdn_z", "hg_q", "hg_f", "hg_i", "hg_g",
             "gla_qk", "gla_v", "gla_gk", "gla_g")
    src = dict(zip(names, np.concatenate([[0], np.cumsum(sizes)[:-1]])))
    wid = dict(zip(names, sizes))
    order = ("fox_qkv", "gdn_qkv", "gdn_z", "hg_q", "hg_f", "hg_i", "hg_g", "gla_qk", "gla_v", "gla_g",
             "fox_f", "gdn_b", "gdn_a", "gla_gk")
    moves, dst = [], 0
    for nm in order:
        moves.append((int(src[nm]), dst, wid[nm]))
        dst += wid[nm]
    return moves, dst


def _regroup_kernel(w_ref, o_ref):
    moves, used = _w_in_moves()
    x = w_ref[0]
    for src, dst, wd in moves:
        o_ref[0, :, dst:dst + wd] = x[:, src:src + wd].astype(BF16)
    o_ref[0, :, used:IN_COLS_PAD] = jnp.zeros((x.shape[0], IN_COLS_PAD - used), BF16)


def _regroup_w_in(w_in):
    depth, d, cols = w_in.shape
    tr = 256
    return pl.pallas_call(
        _regroup_kernel,
        grid=(depth, d // tr),
        in_specs=[pl.BlockSpec((1, tr, cols), lambda l, i: (l, i, 0))],
        out_specs=pl.BlockSpec((1, tr, IN_COLS_PAD), lambda l, i: (l, i, 0)),
        out_shape=jax.ShapeDtypeStruct((depth, d, IN_COLS_PAD), BF16),
        compiler_params=_cparams(2),
        name="regroup_w_in",
    )(w_in)


def _lane_row(vals, lane0, width):
    return jnp.zeros((1, width), F32).at[0, lane0:lane0 + vals.shape[0]].set(vals.astype(F32))


def _tile_heads(g):
    return jnp.tile(g.astype(F32), N_HEADS)[None, :]


def kernel(x, norm1_g, w_in, fox_qn_g, fox_kn_g, fox_b_f, fox_on_g, gdn_conv_w, gdn_a_log, gdn_dt_bias, gdn_on_g,
           hg_lb, hg_on_g, gla_w_gk, gla_b_gk, gla_on_g, w_out, norm2_g, w_up, ffn_conv_w, ffn_conv_b, w_down):
    b, s, d = x.shape
    depth = w_in.shape[0]
    consts = _constants()

    cs = jnp.cumsum(jax.nn.softmax(hg_lb.astype(F32), axis=0), axis=0)
    lower = cs - cs[0:1]

    w_in_all = _regroup_w_in(w_in)
    x2d = x.reshape(b * s, d)
    for l in range(depth):
        fox_qkv, gdn_qkv, gdn_z, hg, gla, gates = _inproj(x2d, norm1_g[l][None, :], w_in_all, l)

        qa, ka, va = _fox_prep(fox_qkv, gates, _lane_row(fox_b_f[l], GATE_FOX_F, GATE_W),
                               _tile_heads(fox_qn_g[l]), _tile_heads(fox_kn_g[l]), consts, b, s)
        o_a = _fox_attn(qa, ka, va, jnp.tile(fox_on_g[l].astype(F32), 2)[None, :], b, s)

        cw = jnp.zeros((8, 3 * GROUP_W), F32).at[0:GDN_CONV].set(gdn_conv_w[l])
        o_b = _gdn(gdn_qkv, gdn_z, gates, cw, _lane_row(gdn_a_log[l], GATE_GDN_A, GATE_W),
                   _lane_row(gdn_dt_bias[l], GATE_GDN_A, GATE_W), _tile_heads(gdn_on_g[l]), consts, b, s)

        lb = lower[l][None, :]
        o_c = _la("hgrn2", hg, (jnp.log(lb), jnp.log1p(-lb), 1.0 - lb), _tile_heads(hg_on_g[l]), consts, b, s)

        wgk = jnp.zeros((GATE_W, GLA_KW), F32).at[GATE_GLA_GK:GATE_GLA_GK + GLA_RANK].set(gla_w_gk[l]).astype(BF16)
        o_d = _la("gla", gla, (gates, wgk, gla_b_gk[l][None, :].astype(F32)), _tile_heads(gla_on_g[l]), consts, b, s)

        wo = w_out[l].astype(BF16)
        taps = jnp.concatenate([ffn_conv_w[l], ffn_conv_b[l][None, :],
                                jnp.zeros((8 - FFN_CONV - 1, 2 * D_FF), F32)], axis=0)
        x2d = _ffn(x2d, (o_a, o_b, o_c, o_d), wo, norm2_g[l][None, :], w_up[l].astype(BF16), taps,
                   w_down[l].astype(BF16), b, s)
    return x2d.reshape(b, s, d)
```

```python
import functools

import numpy as np
import jax
import jax.numpy as jnp
from jax import lax
from jax.experimental import pallas as pl
from jax.experimental.pallas import tpu as pltpu

F32 = jnp.float32
BF16 = jnp.bfloat16

D_MODEL = 1024
N_HEADS = 4
HEAD_DIM = 64
GROUP_W = N_HEADS * HEAD_DIM
GLA_DK = 32
GLA_KW = N_HEADS * GLA_DK
GLA_RANK = 16
GLA_NORM = 16.0
GDN_CONV = 4
D_FF = 2816
FFN_CONV = 3
EPS = 1e-6
CHUNK = 64
WY_GROUP = 4
LA_GROUP_ROWS = {"hgrn2": 256, "gla": 512}

LANES = 128
GATE_W = LANES
GATE_FOX_F, GATE_GDN_B, GATE_GDN_A, GATE_GLA_GK = 0, 4, 8, 12
assert GATE_FOX_F == 0

IN_SEGS = (3 * GROUP_W, 3 * GROUP_W, GROUP_W, 4 * GROUP_W, 3 * GROUP_W, GATE_W)
IN_COLS_PAD = sum(IN_SEGS)

ROW_TILE = 512
FFN_ROW_TILE = 512
FF_CHUNK = 256
FOX_BLOCK = 512
NEG_BIG = -1e30
LOG2_E = 1.4426950408889634

VMEM_LIMIT = 56 * 1024 * 1024


def _dot(a, b):
    return jnp.dot(a, b, preferred_element_type=F32)


def _dot_nt(a, b):
    return lax.dot_general(a, b, (((1,), (1,)), ((), ())), preferred_element_type=F32)


def _dot_tn(a, b):
    return lax.dot_general(a, b, (((0,), (0,)), ((), ())), preferred_element_type=F32)


def _split3(x):
    hi = x.astype(BF16)
    r1 = x - hi.astype(F32)
    mid = r1.astype(BF16)
    lo = (r1 - mid.astype(F32)).astype(BF16)
    return hi, mid, lo


def _dot01_l(m01, x):
    hi, mid, lo = _split3(x)
    return _dot(m01, hi) + _dot(m01, mid) + _dot(m01, lo)


def _dot01_r(x, m01):
    hi, mid, lo = _split3(x)
    return _dot(hi, m01) + _dot(mid, m01) + _dot(lo, m01)


def _seg_meansq(x, bd16, width):
    sq = x * x
    hi = sq.astype(BF16)
    lo = (sq - hi.astype(F32)).astype(BF16)
    return (_dot(hi, bd16) + _dot(lo, bd16)) * (1.0 / width)


def _log_sigmoid(x):
    return jnp.minimum(x, 0.0) - jnp.log1p(jnp.exp(-jnp.abs(x)))


def _softplus(x):
    return jnp.maximum(x, 0.0) + jnp.log1p(jnp.exp(-jnp.abs(x)))


def _silu(x):
    return x * jax.nn.sigmoid(x)


def _tile_rows(x, n):
    return jnp.concatenate([x] * n, axis=0)


def _interleave(*gens):
    live = list(gens)
    while live:
        for gen in list(live):
            try:
                next(gen)
            except StopIteration:
                live.remove(gen)


def _software_pipeline(phases, n_steps):
    def step(t):
        _interleave(*[ph(t - d) for d, ph in enumerate(phases) if isinstance(t, jax.Array) or 0 <= t - d < n_steps])

    depth = len(phases) - 1
    for t in range(n_steps + depth):
        if depth <= t < n_steps:
            if t == depth:
                lax.fori_loop(depth, n_steps, lambda tt, c: (step(tt), c)[1], 0)
        else:
            step(t)


def _full_spec(shape):
    nd = len(shape)
    return pl.BlockSpec(shape, lambda *_: (0,) * nd, pipeline_mode=pl.Buffered(1))


def _cparams(n_axes, flags=None):
    return pltpu.CompilerParams(dimension_semantics=("arbitrary",) * n_axes, vmem_limit_bytes=VMEM_LIMIT,
                                flags=flags)


def _block_mask(rows, row_blk, cols, col_blk):
    r = np.arange(rows)[:, None] // row_blk
    c = np.arange(cols)[None, :] // col_blk
    return (r == c).astype(np.float32)


def _tril_ones(n):
    return np.tril(np.ones((n, n), np.float32))


def _expand_mat(lane0):
    m = np.zeros((GATE_W, GROUP_W), np.float32)
    for h in range(N_HEADS):
        m[lane0 + h, h * HEAD_DIM:(h + 1) * HEAD_DIM] = 1.0
    return m


def _chunk_masks():
    r = np.arange(CHUNK)[:, None]
    s = np.arange(GROUP_W)[None, :] % CHUNK
    return ((r == s).astype(np.float32), (r > s).astype(np.float32), (r >= s).astype(np.float32))


LEVELS = (1, 2, 4, 8, 16, 32)


def _level_masks():
    r = np.arange(CHUNK)[:, None]
    s = np.arange(GROUP_W)[None, :] % CHUNK
    out = []
    for m in LEVELS:
        same = (r // (2 * m)) == (s // (2 * m))
        out.append((same & ((r % (2 * m)) >= m) & ((s % (2 * m)) < m)).astype(np.float32))
    return np.stack(out)


def _fox_routes():
    rq = np.zeros((GATE_W, N_HEADS * LANES), np.float32)
    rk = np.zeros((GATE_W, N_HEADS * LANES), np.float32)
    oq = np.zeros((N_HEADS, 1, LANES), np.float32)
    ok = np.zeros((N_HEADS, 1, LANES), np.float32)
    for h in range(N_HEADS):
        base = (1 - h % 2) * HEAD_DIM
        for j in range(3):
            rq[j * N_HEADS + h, h * LANES + base + j] = 1.0
            rk[j * N_HEADS + h, h * LANES + base + 3 + j] = -1.0
            oq[h, 0, base + 3 + j] = 1.0
            ok[h, 0, base + j] = 1.0
    return rq, rk, oq, ok


def _inproj_kernel(x_ref, g_ref, w_ref, *out_refs):
    x = x_ref[...]
    ms = jnp.mean(x * x, axis=-1, keepdims=True)
    h = (x * lax.rsqrt(ms + EPS) * g_ref[...]).astype(BF16)
    off = 0
    for ref, width in zip(out_refs, IN_SEGS):
        step = min(width, 256)
        for c in range(0, width, step):
            ref[:, c:c + step] = _dot(h, w_ref[:, off + c:off + c + step])
        off += width


def _inproj(x2d, g, w_all, layer):
    t = x2d.shape[0]
    tm = min(ROW_TILE, t)
    return pl.pallas_call(
        _inproj_kernel,
        grid=(t // tm,),
        in_specs=[pl.BlockSpec((tm, D_MODEL), lambda i: (i, 0)),
                  _full_spec((1, D_MODEL)),
                  pl.BlockSpec((None, D_MODEL, IN_COLS_PAD), lambda i: (layer, 0, 0), pipeline_mode=pl.Buffered(1))],
        out_specs=[pl.BlockSpec((tm, wd), lambda i: (i, 0)) for wd in IN_SEGS],
        out_shape=[jax.ShapeDtypeStruct((t, wd), F32) for wd in IN_SEGS],
        compiler_params=_cparams(1),
        name="inproj",
    )(x2d, g, w_all)


def _fox_prep_kernel(qkv_ref, gates_ref, bf_ref, qg_ref, kg_ref, bd_ref, ltri_ref, rq_ref, rk_ref, oq_ref, ok_ref,
                     qa_ref, ka_ref, va_ref, *, rb):
    s_len = qkv_ref.shape[0]
    lane = lax.broadcasted_iota(jnp.int32, (rb, LANES), 1)

    def blk(i, carry):
        r0 = pl.multiple_of(i * rb, rb)
        logf = _log_sigmoid(gates_ref[pl.ds(r0, rb), :] + bf_ref[...])
        c = _dot01_l(ltri_ref[...], logf) + carry
        hi, mid, lo = [t.astype(F32) for t in _split3(c)]
        c3 = jnp.where(lane < N_HEADS, hi, jnp.where(lane < 2 * N_HEADS, pltpu.roll(mid, N_HEADS, 1),
                                                       pltpu.roll(lo, 2 * N_HEADS, 1))).astype(BF16)
        qkv = qkv_ref[pl.ds(r0, rb), :]
        q = qkv[:, 0:GROUP_W]
        k = qkv[:, GROUP_W:2 * GROUP_W]
        v = qkv[:, 2 * GROUP_W:3 * GROUP_W]
        bd = bd_ref[...]
        qn = q * lax.rsqrt(_seg_meansq(q, bd, HEAD_DIM) + EPS) * qg_ref[...] * (HEAD_DIM ** -0.5)
        kn = k * lax.rsqrt(_seg_meansq(k, bd, HEAD_DIM) + EPS) * kg_ref[...]
        q_route = _dot(c3, rq_ref[...])
        k_route = _dot(c3, rk_ref[...])
        for h in range(N_HEADS):
            p = h // 2
            own = (lane // HEAD_DIM) == (h % 2)
            qa = jnp.where(own, qn[:, p * LANES:(p + 1) * LANES], q_route[:, h * LANES:(h + 1) * LANES] + oq_ref[h])
            ka = jnp.where(own, kn[:, p * LANES:(p + 1) * LANES], k_route[:, h * LANES:(h + 1) * LANES] + ok_ref[h])
            va = jnp.where(own, v[:, p * LANES:(p + 1) * LANES], jnp.where(lane == (1 - h % 2) * HEAD_DIM, 1.0, 0.0))
            qa_ref[0, h, pl.ds(r0, rb), :] = qa.astype(BF16)
            ka_ref[0, h, pl.ds(r0, rb), :] = ka.astype(BF16)
            va_ref[0, h, pl.ds(r0, rb), :] = va.astype(BF16)
        return c[rb - 1:rb, :]

    lax.fori_loop(0, s_len // rb, blk, jnp.zeros((1, GATE_W), F32))


def _fox_prep(fox_qkv, gates, bf_row, qg_row, kg_row, consts, b, s):
    rb = min(256, s)
    rq, rk, oq, ok = consts["fox_routes"]
    ltri = jnp.asarray(_tril_ones(rb), BF16)
    kern = functools.partial(_fox_prep_kernel, rb=rb)
    return pl.pallas_call(
        kern,
        grid=(b,),
        in_specs=[pl.BlockSpec((s, 3 * GROUP_W), lambda i: (i, 0)),
                  pl.BlockSpec((s, GATE_W), lambda i: (i, 0)),
                  _full_spec((1, GATE_W)), _full_spec((1, GROUP_W)), _full_spec((1, GROUP_W)),
                  _full_spec((GROUP_W, GROUP_W)), _full_spec((rb, rb)),
                  _full_spec(rq.shape), _full_spec(rk.shape), _full_spec(oq.shape), _full_spec(ok.shape)],
        out_specs=[pl.BlockSpec((1, N_HEADS, s, LANES), lambda i: (i, 0, 0, 0)),
                   pl.BlockSpec((1, N_HEADS, s, LANES), lambda i: (i, 0, 0, 0)),
                   pl.BlockSpec((1, N_HEADS, s, LANES), lambda i: (i, 0, 0, 0))],
        out_shape=[jax.ShapeDtypeStruct((b, N_HEADS, s, LANES), BF16)] * 3,
        compiler_params=_cparams(1),
        name="fox_prep",
    )(fox_qkv, gates, bf_row, qg_row, kg_row, consts["bd256"], ltri, rq, rk, oq, ok)


def _fox_attn_kernel(q_ref, k_ref, v_ref, og_ref, o_ref, s_scr, m_scr, acc_scr, *, blk, nq):
    lane = lax.broadcasted_iota(jnp.int32, (blk, LANES), 1)
    row = lax.broadcasted_iota(jnp.int32, (blk, blk), 0)
    col = lax.broadcasted_iota(jnp.int32, (blk, blk), 1)

    def fold(s):
        m = s[:, 0:LANES]
        for c in range(1, blk // LANES):
            m = jnp.maximum(m, s[:, c * LANES:(c + 1) * LANES])
        return m

    def logit_stages(i):
        slot = i % 2
        for j in range(i + 1):
            for hh in range(2):
                s = _dot_nt(q_ref[0, hh, i * blk:(i + 1) * blk, :], k_ref[0, hh, j * blk:(j + 1) * blk, :])
                if j == i:
                    s = jnp.where(col <= row, s, NEG_BIG)
                s_scr[slot, hh, j] = s
                m_scr[slot, hh] = fold(s) if j == 0 else jnp.maximum(m_scr[slot, hh], fold(s))
            yield

    def value_stages(i):
        slot = i % 2
        row_max = [jnp.max(m_scr[slot, hh], axis=-1, keepdims=True) for hh in range(2)]
        for j in range(i + 1):
            for hh in range(2):
                p = jnp.exp(s_scr[slot, hh, j] - row_max[hh])
                pv = _dot(p.astype(BF16), v_ref[0, hh, j * blk:(j + 1) * blk, :])
                acc_scr[slot, hh] = pv if j == 0 else acc_scr[slot, hh] + pv
            yield
        outs = []
        for hh in range(2):
            acc = acc_scr[slot, hh]
            l = jnp.sum(jnp.where(lane == (1 - hh) * HEAD_DIM, acc, 0.0), axis=-1, keepdims=True)
            o = acc / l
            own = (lane // HEAD_DIM) == hh
            ms = jnp.sum(jnp.where(own, o * o, 0.0), axis=-1, keepdims=True) * (1.0 / HEAD_DIM)
            outs.append(o * lax.rsqrt(ms + EPS) * og_ref[...])
        o_ref[i * blk:(i + 1) * blk, :] = jnp.where((lane // HEAD_DIM) == 0, outs[0], outs[1]).astype(BF16)

    _interleave(logit_stages(0))
    for i in range(nq):
        if i + 1 < nq:
            _interleave(value_stages(i), logit_stages(i + 1))
        else:
            _interleave(value_stages(i))


def _fox_attn(qa, ka, va, og_row, b, s):
    blk = min(FOX_BLOCK, s)
    nq = s // blk
    kern = functools.partial(_fox_attn_kernel, blk=blk, nq=nq)
    head_pair = pl.BlockSpec((1, 2, s, LANES), lambda i, p: (i, p, 0, 0))
    return pl.pallas_call(
        kern,
        grid=(b, 2),
        in_specs=[head_pair, head_pair, head_pair, _full_spec((1, LANES))],
        out_specs=pl.BlockSpec((s, LANES), lambda i, p: (i, p)),
        out_shape=jax.ShapeDtypeStruct((b * s, GROUP_W), BF16),
        scratch_shapes=[pltpu.VMEM((2, 2, nq, blk, blk), F32),
                        pltpu.VMEM((2, 2, blk, LANES), F32),
                        pltpu.VMEM((2, 2, blk, LANES), F32)],
        compiler_params=_cparams(2),
        name="fox_attn",
    )(qa, ka, va, og_row)


def _gdn_kernel(qkv_ref, z_ref, gates_ref, cw_ref, alog_ref, dt_ref, og_ref, bd_ref, eb_ref, eg_ref,
                eye_ref, strict_ref, causal_ref, ones_ref, ctri_ref, lvl_ref,
                o_ref, xpad, q_scr, k_scr, v_scr, beta_scr, g_scr, o_scr, u_scr, s_scr,
                wq_scr, aqk_scr, kout_scr, *, rb):
    s_len = qkv_ref.shape[0]
    nblk = s_len // rb
    bd16 = bd_ref[...]
    bdf = bd16.astype(F32)

    xpad[0:8, :] = jnp.zeros((8, 3 * GROUP_W), F32)
    xpad[8:, :] = qkv_ref[...]

    def prep_stages(i):
        r0 = pl.multiple_of(i * rb, rb)

        gt = gates_ref[pl.ds(r0, rb), :]
        beta_scr[pl.ds(r0, rb), :] = _dot01_r(jax.nn.sigmoid(gt), eb_ref[...])
        g_s = -jnp.exp(alog_ref[...]) * _softplus(gt + dt_ref[...])
        g_rep = [_dot(t, eg_ref[...]).astype(BF16) for t in _split3(g_s)]

        def conv_silu(c0):
            xx = xpad[pl.ds(r0, rb + 8), c0:c0 + LANES]
            y = cw_ref[GDN_CONV - 1:GDN_CONV, c0:c0 + LANES] * xx[8:8 + rb]
            for j in range(GDN_CONV - 1):
                y = y + cw_ref[j:j + 1, c0:c0 + LANES] * pltpu.roll(xx, GDN_CONV - 1 - j, 0)[8:8 + rb]
            return _silu(y)

        yield
        q_lo = conv_silu(0)
        yield
        q = jnp.concatenate([q_lo, conv_silu(LANES)], axis=1)
        q_ss = _seg_meansq(q, bd16, 1.0)
        yield
        k_lo = conv_silu(2 * LANES)
        yield
        k = jnp.concatenate([k_lo, conv_silu(3 * LANES)], axis=1)
        k_ss = _seg_meansq(k, bd16, 1.0)
        yield
        v_scr[pl.ds(r0, rb), 0:LANES] = conv_silu(4 * LANES)
        yield
        v_scr[pl.ds(r0, rb), LANES:2 * LANES] = conv_silu(5 * LANES)
        ct = ctri_ref[...]
        g_scr[pl.ds(r0, rb), :] = _dot(ct, g_rep[0]) + _dot(ct, g_rep[1]) + _dot(ct, g_rep[2])
        yield
        q_scr[pl.ds(r0, rb), :] = q * lax.rsqrt(q_ss + EPS) * (HEAD_DIM ** -0.5)
        k_scr[pl.ds(r0, rb), :] = k * lax.rsqrt(k_ss + EPS)

    def bd(y16):
        return _tile_rows(y16, N_HEADS) * bd16

    eye = eye_ref[...]

    def wy_stages(i):
        ns = [i * WY_GROUP + c for c in range(WY_GROUP)]
        r0s = [pl.multiple_of(n * CHUNK, CHUNK) for n in ns]
        q = [q_scr[pl.ds(r0, CHUNK), :] for r0 in r0s]
        k = [k_scr[pl.ds(r0, CHUNK), :] for r0 in r0s]
        beta = [beta_scr[pl.ds(r0, CHUNK), :] for r0 in r0s]
        g = [g_scr[pl.ds(r0, CHUNK), :] for r0 in r0s]
        kb = [kc * bc for kc, bc in zip(k, beta)]
        aa = [_dot_nt(jnp.concatenate([kbc.astype(BF16), qc.astype(BF16)], axis=0), bd(kc.astype(BF16)))
              for kbc, qc, kc in zip(kb, q, k)]
        g_row = [_dot01_l(ones_ref[...], gc * eye) for gc in g]
        yield
        decay = [jnp.exp(jnp.minimum(gc - grc, 0.0)) for gc, grc in zip(g, g_row)]
        m = [ac[0:CHUNK] * dc * strict_ref[...] for ac, dc in zip(aa, decay)]
        for r0, ac, dc in zip(r0s, aa, decay):
            aqk_scr[pl.ds(r0, CHUNK), :] = (ac[CHUNK:2 * CHUNK] * dc * causal_ref[...]).astype(BF16)
        pm = [eye - mc * lvl_ref[0] for mc in m]
        for li in range(1, len(LEVELS)):
            x16 = [pc.astype(BF16) for pc in pm]
            xc = [_dot(xc16, bd((mc * lvl_ref[li]).astype(BF16))) for xc16, mc in zip(x16, m)]
            yield
            pm = [pc - _dot(t.astype(BF16), bd(xc16)) for pc, t, xc16 in zip(pm, xc, x16)]
            yield
        t16 = [pc.astype(BF16) for pc in pm]
        eg = [jnp.exp(gc) for gc in g]
        for c, (n, r0) in enumerate(zip(ns, r0s)):
            v = v_scr[pl.ds(r0, CHUNK), :]
            g_last = g_scr[pl.ds(r0 + CHUNK - 1, 1), :]
            wq_scr[n, 0:CHUNK, :] = _dot(t16[c], bd((kb[c] * eg[c]).astype(BF16))).astype(BF16)
            wq_scr[n, CHUNK:2 * CHUNK, :] = (q[c] * eg[c]).astype(BF16)
            u_scr[pl.ds(r0, CHUNK), :] = _dot(t16[c], bd((v * beta[c]).astype(BF16)))
            kout_scr[pl.ds(r0, CHUNK), :] = (k[c] * jnp.exp(g_last - g[c])).astype(BF16)

    def scan_stages(i):
        st = s_scr[...]
        for c in range(WY_GROUP):
            n = i * WY_GROUP + c
            r0 = pl.multiple_of(n * CHUNK, CHUNK)
            wq = _dot(wq_scr[n], st.astype(BF16))
            yield
            v16 = (u_scr[pl.ds(r0, CHUNK), :] - wq[0:CHUNK]).astype(BF16)
            a_last = jnp.exp(g_scr[pl.ds(r0 + CHUNK - 1, 1), :])
            st = st * a_last + _dot_tn(kout_scr[pl.ds(r0, CHUNK), :], v16) * bdf
            o_scr[pl.ds(r0, CHUNK), :] = wq[CHUNK:2 * CHUNK] + _dot(aqk_scr[pl.ds(r0, CHUNK), :], bd(v16))
            yield
        s_scr[...] = st

    def out_stages(i):
        r0 = pl.multiple_of(i * rb, rb)
        o = o_scr[pl.ds(r0, rb), :]
        o_ms = _seg_meansq(o, bd16, HEAD_DIM)
        yield
        on = o * lax.rsqrt(o_ms + EPS) * og_ref[...]
        o_ref[pl.ds(r0, rb), :] = (on * _silu(z_ref[pl.ds(r0, rb), :])).astype(BF16)

    s_scr[...] = jnp.zeros((GROUP_W, GROUP_W), F32)
    _software_pipeline((prep_stages, wy_stages, scan_stages, out_stages), nblk)


def _gdn(gdn_qkv, gdn_z, gates, cw, alog_row, dt_row, og_row, consts, b, s):
    rb = min(256, s)
    eye, strict, causal = consts["chunk_masks"]
    kern = functools.partial(_gdn_kernel, rb=rb)
    row = lambda w: pl.BlockSpec((s, w), lambda i: (i, 0))
    return pl.pallas_call(
        kern,
        grid=(b,),
        in_specs=[row(3 * GROUP_W), row(GROUP_W), row(GATE_W),
                  _full_spec((8, 3 * GROUP_W)), _full_spec((1, GATE_W)), _full_spec((1, GATE_W)),
                  _full_spec((1, GROUP_W)), _full_spec((GROUP_W, GROUP_W)),
                  _full_spec((GATE_W, GROUP_W)), _full_spec((GATE_W, GROUP_W)),
                  _full_spec((CHUNK, GROUP_W)), _full_spec((CHUNK, GROUP_W)), _full_spec((CHUNK, GROUP_W)),
                  _full_spec((CHUNK, CHUNK)), _full_spec((rb, rb)), _full_spec((len(LEVELS), CHUNK, GROUP_W))],
        out_specs=row(GROUP_W),
        out_shape=jax.ShapeDtypeStruct((b * s, GROUP_W), BF16),
        scratch_shapes=[pltpu.VMEM((s + 8, 3 * GROUP_W), F32)]
                       + [pltpu.VMEM((s, GROUP_W), F32) for _ in range(7)]
                       + [pltpu.VMEM((GROUP_W, GROUP_W), F32),
                          pltpu.VMEM((s // CHUNK, 2 * CHUNK, GROUP_W), BF16),
                          pltpu.VMEM((s, GROUP_W), BF16), pltpu.VMEM((s, GROUP_W), BF16)],
        compiler_params=_cparams(1),
        name="gdn",
    )(gdn_qkv, gdn_z, gates, cw, alog_row, dt_row, og_row, consts["bd256"], consts["expand_b"], consts["expand_g"],
      eye, strict, causal, consts["ones64"],
      jnp.asarray(_tril_ones(rb) * _block_mask(rb, CHUNK, rb, CHUNK), BF16), consts["level_masks"])


def _la_kernel(*refs, variant, rb):
    if variant == "hgrn2":
        (x_ref, la_ref, l1_ref, oml_ref, og_ref, bdv_ref, bdk_ref, eye_ref, lvl_ref, ltri_ref,
         o_ref, q_scr, k_scr, a_scr, v_scr, gate_scr, gc_scr, o_scr, st_scr) = refs
        kw, dk = GROUP_W, HEAD_DIM
    else:
        (x_ref, gates_ref, wgk_ref, bgk_ref, og_ref, bdv_ref, bdk_ref, eye_ref, lvl_ref, ltri_ref,
         o_ref, q_scr, k_scr, a_scr, v_scr, gate_scr, gc_scr, o_scr, st_scr) = refs
        kw, dk = GLA_KW, GLA_DK
    s_len = x_ref.shape[0]
    nblk = s_len // rb
    bdv16 = bdv_ref[...]
    bdk16 = bdk_ref[...]
    bdkf = bdk16.astype(F32)

    def prep_stages(i):
        r0 = pl.multiple_of(i * rb, rb)
        if variant == "hgrn2":
            f_logit = x_ref[pl.ds(r0, rb), GROUP_W:2 * GROUP_W]
            q_scr[pl.ds(r0, rb), :] = _silu(x_ref[pl.ds(r0, rb), 0:GROUP_W]) * (dk ** -0.5)
            yield
            a = la_ref[...]
            bterm = l1_ref[...] + _log_sigmoid(f_logit)
            amax = jnp.maximum(a, bterm)
            a_scr[pl.ds(r0, rb), :] = amax + jnp.log1p(jnp.exp(-jnp.abs(a - bterm)))
            yield
            k_scr[pl.ds(r0, rb), :] = oml_ref[...] * jax.nn.sigmoid(-f_logit)
            yield
            v_scr[pl.ds(r0, rb), :] = x_ref[pl.ds(r0, rb), 2 * GROUP_W:3 * GROUP_W]
            gate_scr[pl.ds(r0, rb), :] = x_ref[pl.ds(r0, rb), 3 * GROUP_W:4 * GROUP_W]
        else:
            lr = _dot(gates_ref[pl.ds(r0, rb), :].astype(BF16), wgk_ref[...]) + bgk_ref[...]
            q_scr[pl.ds(r0, rb), :] = x_ref[pl.ds(r0, rb), 0:kw] * (dk ** -0.5)
            k_scr[pl.ds(r0, rb), :] = x_ref[pl.ds(r0, rb), kw:2 * kw]
            yield
            v_scr[pl.ds(r0, rb), :] = x_ref[pl.ds(r0, rb), 2 * kw:2 * kw + GROUP_W]
            gate_scr[pl.ds(r0, rb), :] = x_ref[pl.ds(r0, rb), 2 * kw + GROUP_W:2 * kw + 2 * GROUP_W]
            yield
            a_scr[pl.ds(r0, rb), :] = _log_sigmoid(lr) * (1.0 / GLA_NORM)

    sub = lax.broadcasted_iota(jnp.int32, (8, kw), 0)

    def level_ref(m, slot):
        def brow(r):
            return jnp.broadcast_to(gc_scr[slot, r:r + 1, :], (8, kw))
        pieces = []
        for a in range(CHUNK // 8):
            if 2 * m >= 8:
                pieces.append(brow((8 * a) // (2 * m) * (2 * m) + m - 1))
            elif m == 2:
                pieces.append(jnp.where(sub < 4, brow(8 * a + 1), brow(8 * a + 5)))
            else:
                p = jnp.where(sub < 2, brow(8 * a), brow(8 * a + 2))
                p = jnp.where(sub < 4, p, jnp.where(sub < 6, brow(8 * a + 4), brow(8 * a + 6)))
                pieces.append(p)
        return jnp.concatenate(pieces, axis=0)

    def chunk_stages(i):
        slots = range(rb // CHUNK)
        r0s = [pl.multiple_of(i * rb + c * CHUNK, CHUNK) for c in slots]
        q = [q_scr[pl.ds(r0, CHUNK), :] for r0 in r0s]
        k = [k_scr[pl.ds(r0, CHUNK), :] for r0 in r0s]
        v16 = [v_scr[pl.ds(r0, CHUNK), :].astype(BF16) for r0 in r0s]
        g = [_dot01_l(ltri_ref[...], a_scr[pl.ds(r0, CHUNK), :]) * LOG2_E for r0 in r0s]
        yield
        for c in slots:
            gc_scr[c] = g[c]
        q16 = [qc.astype(BF16) for qc in q]
        k16 = [kc.astype(BF16) for kc in k]
        a_in = [_dot_nt(q16[c], _tile_rows(k16[c], N_HEADS) * bdk16) * eye_ref[...] for c in slots]
        yield
        for li, m in enumerate(LEVELS):
            for c in slots:
                e = jnp.exp2(-jnp.abs(g[c] - level_ref(m, c))).astype(BF16)
                a_in[c] = a_in[c] + _dot_nt(q16[c] * e, _tile_rows(k16[c] * e, N_HEADS) * bdk16) * lvl_ref[li]
            yield
        o = [_dot(a_in[c].astype(BF16), _tile_rows(v16[c], N_HEADS) * bdv16) for c in slots]
        ds = [_dot_tn(v16[c], (k[c] * jnp.exp2(g[c][CHUNK - 1:CHUNK, :] - g[c])).astype(BF16)) * bdkf for c in slots]
        q_in = [(q[c] * jnp.exp2(g[c])).astype(BF16) for c in slots]
        yield
        st = st_scr[...]
        for c in slots:
            o_scr[pl.ds(r0s[c], CHUNK), :] = o[c] + _dot_nt(q_in[c], st.astype(BF16))
            st = st * jnp.exp2(g[c][CHUNK - 1:CHUNK, :]) + ds[c]
        st_scr[...] = st

    def out_stages(i):
        r0 = pl.multiple_of(i * rb, rb)
        o = o_scr[pl.ds(r0, rb), :]
        o_ms = _seg_meansq(o, bdv16, HEAD_DIM)
        yield
        on = o * lax.rsqrt(o_ms + EPS) * og_ref[...]
        o_ref[pl.ds(r0, rb), :] = (on * _silu(gate_scr[pl.ds(r0, rb), :])).astype(BF16)

    st_scr[...] = jnp.zeros((GROUP_W, kw), F32)
    _software_pipeline((prep_stages, chunk_stages, out_stages), nblk)


def _la(variant, x, extra, og_row, consts, b, s):
    rb = min(LA_GROUP_ROWS[variant], s)
    kw = GROUP_W if variant == "hgrn2" else GLA_KW
    xw = 4 * GROUP_W if variant == "hgrn2" else 3 * GROUP_W
    kern = functools.partial(_la_kernel, variant=variant, rb=rb)
    row = lambda w: pl.BlockSpec((s, w), lambda i: (i, 0))
    if variant == "hgrn2":
        in_specs = [row(xw)] + [_full_spec((1, GROUP_W))] * 3
        bdk = consts["bd256"]
    else:
        in_specs = [row(xw), row(GATE_W), _full_spec((GATE_W, GLA_KW)), _full_spec((1, GLA_KW))]
        bdk = consts["bdk_gla"]
    in_specs += [_full_spec((1, GROUP_W)), _full_spec((GROUP_W, GROUP_W)), _full_spec((GROUP_W, kw)),
                 _full_spec((CHUNK, GROUP_W)), _full_spec((len(LEVELS), CHUNK, GROUP_W)), _full_spec((CHUNK, CHUNK))]
    return pl.pallas_call(
        kern,
        grid=(b,),
        in_specs=in_specs,
        out_specs=row(GROUP_W),
        out_shape=jax.ShapeDtypeStruct((b * s, GROUP_W), BF16),
        scratch_shapes=[pltpu.VMEM((s, kw), F32) for _ in range(3)]
                       + [pltpu.VMEM((s, GROUP_W), F32) for _ in range(2)]
                       + [pltpu.VMEM((rb // CHUNK, CHUNK, kw), F32), pltpu.VMEM((s, GROUP_W), F32),
                          pltpu.VMEM((GROUP_W, kw), F32)],
        compiler_params=_cparams(1),
        name="la_" + variant,
    )(x, *extra, og_row, consts["bd256"], bdk, consts["chunk_masks"][0], consts["level_masks"], consts["ltri64"])


FFN_TAIL = 16
FFN_LATE_CHUNKS = 2


def _ffn_kernel(x_ref, oa_ref, ob_ref, oc_ref, od_ref, wo_ref, g2_ref, wup_ref, ctap_ref, wd_ref,
                out_ref, hext, act_scr, *, tm):
    it = pl.program_id(1)
    mixed = jnp.concatenate([oa_ref[...], ob_ref[...], oc_ref[...], od_ref[...]], axis=1)
    x1 = x_ref[...] + _dot(mixed, wo_ref[...])
    out_ref[...] = x1
    ms = jnp.mean(x1 * x1, axis=-1, keepdims=True)
    h2 = (x1 * lax.rsqrt(ms + EPS) * g2_ref[...]).astype(BF16)

    @pl.when(it == 0)
    def _():
        hext[0:FFN_TAIL, :] = jnp.zeros((FFN_TAIL, D_MODEL), BF16)

    hext[FFN_TAIL:, :] = h2

    def conv(col0):
        u = _dot(hext[...], wup_ref[:, col0:col0 + FF_CHUNK])
        c = ctap_ref[:, col0:col0 + FF_CHUNK]
        r1 = pltpu.roll(u, 1, 0)
        r2 = pltpu.roll(u, 2, 0)
        y = c[0:1, :] * r2[FFN_TAIL:FFN_TAIL + tm] + c[1:2, :] * r1[FFN_TAIL:FFN_TAIL + tm]
        return y + c[2:3, :] * u[FFN_TAIL:FFN_TAIL + tm] + c[3:4, :]

    for j in range(D_FF // FF_CHUNK):
        gate = conv(j * FF_CHUNK)
        up = conv(D_FF + j * FF_CHUNK)
        act_scr[:, j * FF_CHUNK:(j + 1) * FF_CHUNK] = (_silu(gate) * up).astype(BF16)
    k1 = (D_FF // FF_CHUNK - FFN_LATE_CHUNKS) * FF_CHUNK
    out_ref[...] += _dot(act_scr[:, 0:k1], wd_ref[0:k1, :])
    out_ref[...] += _dot(act_scr[:, k1:D_FF], wd_ref[k1:D_FF, :])
    hext[0:FFN_TAIL, :] = hext[tm:tm + FFN_TAIL, :]


def _ffn(x2d, outs, wo, g2, wup, ctap, wd, b, s):
    tm = min(FFN_ROW_TILE, s)
    nt = s // tm
    kern = functools.partial(_ffn_kernel, tm=tm)
    row = lambda w: pl.BlockSpec((tm, w), lambda i, j: (i * nt + j, 0))
    return pl.pallas_call(
        kern,
        grid=(b, nt),
        in_specs=[row(D_MODEL)] + [row(GROUP_W)] * 4
                 + [_full_spec((D_MODEL, D_MODEL)), _full_spec((1, D_MODEL)),
                    _full_spec((D_MODEL, 2 * D_FF)), _full_spec((8, 2 * D_FF)), _full_spec((D_FF, D_MODEL))],
        out_specs=row(D_MODEL),
        out_shape=jax.ShapeDtypeStruct(x2d.shape, F32),
        scratch_shapes=[pltpu.VMEM((tm + FFN_TAIL, D_MODEL), BF16),
                        pltpu.VMEM((tm, D_FF), BF16)],
        compiler_params=_cparams(2),
        name="outproj_ffn",
    )(x2d, *outs, wo, g2, wup, ctap, wd)


def _constants():
    eye, strict, causal = _chunk_masks()
    return {
        "bd256": jnp.asarray(_block_mask(GROUP_W, HEAD_DIM, GROUP_W, HEAD_DIM), BF16),
        "bdk_gla": jnp.asarray(_block_mask(GROUP_W, HEAD_DIM, GLA_KW, GLA_DK), BF16),
        "expand_b": jnp.asarray(_expand_mat(GATE_GDN_B), BF16),
        "expand_g": jnp.asarray(_expand_mat(GATE_GDN_A), BF16),
        "chunk_masks": (jnp.asarray(eye), jnp.asarray(strict), jnp.asarray(causal)),
        "level_masks": jnp.asarray(_level_masks()),
        "ones64": jnp.ones((CHUNK, CHUNK), BF16),
        "ltri64": jnp.asarray(_tril_ones(CHUNK), BF16),
        "fox_routes": tuple(jnp.asarray(a, BF16) if a.ndim == 2 else jnp.asarray(a) for a in _fox_routes()),
    }


def _w_in_moves():
    sizes = (3 * GROUP_W, N_HEADS, 3 * GROUP_W, N_HEADS, N_HEADS, GROUP_W,
             GROUP_W, GROUP_W, GROUP_W, GROUP_W, 2 * GLA_KW, GROUP_W, GLA_RANK, GROUP_W)
    names = ("fox_qkv", "fox_f", "gdn_qkv", "gdn_b", "gdn_a", "gdn_z", "hg_q", "hg_f", "hg_i", "hg_g",
             "gla_qk", "gla_v", "gla_gk", "gla_g")
    src = dict(zip(names, np.concatenate([[0], np.cumsum(sizes)[:-1]])))
    wid = dict(zip(names, sizes))
    order = ("fox_qkv", "gdn_qkv", "gdn_z", "hg_q", "hg_f", "hg_i", "hg_g", "gla_qk", "gla_v", "gla_g",
             "fox_f", "gdn_b", "gdn_a", "gla_gk")
    moves, dst = [], 0
    for nm in order:
        moves.append((int(src[nm]), dst, wid[nm]))
        dst += wid[nm]
    return moves, dst


def _regroup_kernel(w_ref, o_ref):
    moves, used = _w_in_moves()
    x = w_ref[0]
    for src, dst, wd in moves:
        o_ref[0, :, dst:dst + wd] = x[:, src:src + wd].astype(BF16)
    o_ref[0, :, used:IN_COLS_PAD] = jnp.zeros((x.shape[0], IN_COLS_PAD - used), BF16)


def _regroup_w_in(w_in):
    depth, d, cols = w_in.shape
    tr = 256
    return pl.pallas_call(
        _regroup_kernel,
        grid=(depth, d // tr),
        in_specs=[pl.BlockSpec((1, tr, cols), lambda l, i: (l, i, 0))],
        out_specs=pl.BlockSpec((1, tr, IN_COLS_PAD), lambda l, i: (l, i, 0)),
        out_shape=jax.ShapeDtypeStruct((depth, d, IN_COLS_PAD), BF16),
        compiler_params=_cparams(2),
        name="regroup_w_in",
    )(w_in)


def _lane_row(vals, lane0, width):
    return jnp.zeros((1, width), F32).at[0, lane0:lane0 + vals.shape[0]].set(vals.astype(F32))


def _tile_heads(g):
    return jnp.tile(g.astype(F32), N_HEADS)[None, :]


def kernel(x, norm1_g, w_in, fox_qn_g, fox_kn_g, fox_b_f, fox_on_g, gdn_conv_w, gdn_a_log, gdn_dt_bias, gdn_on_g,
           hg_lb, hg_on_g, gla_w_gk, gla_b_gk, gla_on_g, w_out, norm2_g, w_up, ffn_conv_w, ffn_conv_b, w_down):
    b, s, d = x.shape
    depth = w_in.shape[0]
    consts = _constants()

    cs = jnp.cumsum(jax.nn.softmax(hg_lb.astype(F32), axis=0), axis=0)
    lower = cs - cs[0:1]

    w_in_all = _regroup_w_in(w_in)
    x2d = x.reshape(b * s, d)
    for l in range(depth):
        fox_qkv, gdn_qkv, gdn_z, hg, gla, gates = _inproj(x2d, norm1_g[l][None, :], w_in_all, l)

        qa, ka, va = _fox_prep(fox_qkv, gates, _lane_row(fox_b_f[l], GATE_FOX_F, GATE_W),
                               _tile_heads(fox_qn_g[l]), _tile_heads(fox_kn_g[l]), consts, b, s)
        o_a = _fox_attn(qa, ka, va, jnp.tile(fox_on_g[l].astype(F32), 2)[None, :], b, s)

        cw = jnp.zeros((8, 3 * GROUP_W), F32).at[0:GDN_CONV].set(gdn_conv_w[l])
        o_b = _gdn(gdn_qkv, gdn_z, gates, cw, _lane_row(gdn_a_log[l], GATE_GDN_A, GATE_W),
                   _lane_row(gdn_dt_bias[l], GATE_GDN_A, GATE_W), _tile_heads(gdn_on_g[l]), consts, b, s)

        lb = lower[l][None, :]
        o_c = _la("hgrn2", hg, (jnp.log(lb), jnp.log1p(-lb), 1.0 - lb), _tile_heads(hg_on_g[l]), consts, b, s)

        wgk = jnp.zeros((GATE_W, GLA_KW), F32).at[GATE_GLA_GK:GATE_GLA_GK + GLA_RANK].set(gla_w_gk[l]).astype(BF16)
        o_d = _la("gla", gla, (gates, wgk, gla_b_gk[l][None, :].astype(F32)), _tile_heads(gla_on_g[l]), consts, b, s)

        wo = w_out[l].astype(BF16)
        taps = jnp.concatenate([ffn_conv_w[l], ffn_conv_b[l][None, :],
                                jnp.zeros((8 - FFN_CONV - 1, 2 * D_FF), F32)], axis=0)
        x2d = _ffn(x2d, (o_a, o_b, o_c, o_d), wo, norm2_g[l][None, :], w_up[l].astype(BF16), taps,
                   w_down[l].astype(BF16), b, s)
    return x2d.reshape(b, s, d)
```

```python
import functools

import numpy as np
import jax
import jax.numpy as jnp
from jax import lax
from jax.experimental import pallas as pl
from jax.experimental.pallas import tpu as pltpu

F32 = jnp.float32
BF16 = jnp.bfloat16

D_MODEL = 1024
N_HEADS = 4
HEAD_DIM = 64
GROUP_W = N_HEADS * HEAD_DIM
GLA_DK = 32
GLA_KW = N_HEADS * GLA_DK
GLA_RANK = 16
GLA_NORM = 16.0
GDN_CONV = 4
D_FF = 2816
FFN_CONV = 3
EPS = 1e-6
CHUNK = 64
WY_GROUP = 4
LA_GROUP_ROWS = {"hgrn2": 256, "gla": 512}

LANES = 128
GATE_W = LANES
GATE_FOX_F, GATE_GDN_B, GATE_GDN_A, GATE_GLA_GK = 0, 4, 8, 12
assert GATE_FOX_F == 0

IN_SEGS = (3 * GROUP_W, 3 * GROUP_W, GROUP_W, 4 * GROUP_W, 3 * GROUP_W, GATE_W)
IN_COLS_PAD = sum(IN_SEGS)

ROW_TILE = 512
FFN_ROW_TILE = 512
FF_CHUNK = 256
FOX_BLOCK = 512
NEG_BIG = -1e30
LOG2_E = 1.4426950408889634

VMEM_LIMIT = 56 * 1024 * 1024


def _dot(a, b):
    return jnp.dot(a, b, preferred_element_type=F32)


def _dot_nt(a, b):
    return lax.dot_general(a, b, (((1,), (1,)), ((), ())), preferred_element_type=F32)


def _dot_tn(a, b):
    return lax.dot_general(a, b, (((0,), (0,)), ((), ())), preferred_element_type=F32)


def _split3(x):
    hi = x.astype(BF16)
    r1 = x - hi.astype(F32)
    mid = r1.astype(BF16)
    lo = (r1 - mid.astype(F32)).astype(BF16)
    return hi, mid, lo


def _dot01_l(m01, x):
    hi, mid, lo = _split3(x)
    return _dot(m01, hi) + _dot(m01, mid) + _dot(m01, lo)


def _dot01_r(x, m01):
    hi, mid, lo = _split3(x)
    return _dot(hi, m01) + _dot(mid, m01) + _dot(lo, m01)


def _seg_meansq(x, bd16, width):
    sq = x * x
    hi = sq.astype(BF16)
    lo = (sq - hi.astype(F32)).astype(BF16)
    return (_dot(hi, bd16) + _dot(lo, bd16)) * (1.0 / width)


def _log_sigmoid(x):
    return jnp.minimum(x, 0.0) - jnp.log1p(jnp.exp(-jnp.abs(x)))


def _softplus(x):
    return jnp.maximum(x, 0.0) + jnp.log1p(jnp.exp(-jnp.abs(x)))


def _silu(x):
    return x * jax.nn.sigmoid(x)


def _tile_rows(x, n):
    return jnp.concatenate([x] * n, axis=0)


def _interleave(*gens):
    live = list(gens)
    while live:
        for gen in list(live):
            try:
                next(gen)
            except StopIteration:
                live.remove(gen)


def _software_pipeline(phases, n_steps):
    def step(t):
        _interleave(*[ph(t - d) for d, ph in enumerate(phases) if isinstance(t, jax.Array) or 0 <= t - d < n_steps])

    depth = len(phases) - 1
    for t in range(n_steps + depth):
        if depth <= t < n_steps:
            if t == depth:
                lax.fori_loop(depth, n_steps, lambda tt, c: (step(tt), c)[1], 0)
        else:
            step(t)


def _full_spec(shape):
    nd = len(shape)
    return pl.BlockSpec(shape, lambda *_: (0,) * nd, pipeline_mode=pl.Buffered(1))


def _cparams(n_axes, flags=None):
    return pltpu.CompilerParams(dimension_semantics=("arbitrary",) * n_axes, vmem_limit_bytes=VMEM_LIMIT,
                                flags=flags)


def _block_mask(rows, row_blk, cols, col_blk):
    r = np.arange(rows)[:, None] // row_blk
    c = np.arange(cols)[None, :] // col_blk
    return (r == c).astype(np.float32)


def _tril_ones(n):
    return np.tril(np.ones((n, n), np.float32))


def _expand_mat(lane0):
    m = np.zeros((GATE_W, GROUP_W), np.float32)
    for h in range(N_HEADS):
        m[lane0 + h, h * HEAD_DIM:(h + 1) * HEAD_DIM] = 1.0
    return m


def _chunk_masks():
    r = np.arange(CHUNK)[:, None]
    s = np.arange(GROUP_W)[None, :] % CHUNK
    return ((r == s).astype(np.float32), (r > s).astype(np.float32), (r >= s).astype(np.float32))


LEVELS = (1, 2, 4, 8, 16, 32)


def _level_masks():
    r = np.arange(CHUNK)[:, None]
    s = np.arange(GROUP_W)[None, :] % CHUNK
    out = []
    for m in LEVELS:
        same = (r // (2 * m)) == (s // (2 * m))
        out.append((same & ((r % (2 * m)) >= m) & ((s % (2 * m)) < m)).astype(np.float32))
    return np.stack(out)


def _fox_routes():
    rq = np.zeros((GATE_W, N_HEADS * LANES), np.float32)
    rk = np.zeros((GATE_W, N_HEADS * LANES), np.float32)
    oq = np.zeros((N_HEADS, 1, LANES), np.float32)
    ok = np.zeros((N_HEADS, 1, LANES), np.float32)
    for h in range(N_HEADS):
        base = (1 - h % 2) * HEAD_DIM
        for j in range(3):
            rq[j * N_HEADS + h, h * LANES + base + j] = 1.0
            rk[j * N_HEADS + h, h * LANES + base + 3 + j] = -1.0
            oq[h, 0, base + 3 + j] = 1.0
            ok[h, 0, base + j] = 1.0
    return rq, rk, oq, ok


def _inproj_kernel(x_ref, g_ref, w_ref, *out_refs):
    x = x_ref[...]
    ms = jnp.mean(x * x, axis=-1, keepdims=True)
    h = (x * lax.rsqrt(ms + EPS) * g_ref[...]).astype(BF16)
    off = 0
    for ref, width in zip(out_refs, IN_SEGS):
        step = min(width, 256)
        for c in range(0, width, step):
            ref[:, c:c + step] = _dot(h, w_ref[:, off + c:off + c + step])
        off += width


def _inproj(x2d, g, w_all, layer):
    t = x2d.shape[0]
    tm = min(ROW_TILE, t)
    return pl.pallas_call(
        _inproj_kernel,
        grid=(t // tm,),
        in_specs=[pl.BlockSpec((tm, D_MODEL), lambda i: (i, 0)),
                  _full_spec((1, D_MODEL)),
                  pl.BlockSpec((None, D_MODEL, IN_COLS_PAD), lambda i: (layer, 0, 0), pipeline_mode=pl.Buffered(1))],
        out_specs=[pl.BlockSpec((tm, wd), lambda i: (i, 0)) for wd in IN_SEGS],
        out_shape=[jax.ShapeDtypeStruct((t, wd), F32) for wd in IN_SEGS],
        compiler_params=_cparams(1),
        name="inproj",
    )(x2d, g, w_all)


def _fox_prep_kernel(qkv_ref, gates_ref, bf_ref, qg_ref, kg_ref, bd_ref, ltri_ref, rq_ref, rk_ref, oq_ref, ok_ref,
                     qa_ref, ka_ref, va_ref, *, rb):
    s_len = qkv_ref.shape[0]
    lane = lax.broadcasted_iota(jnp.int32, (rb, LANES), 1)

    def blk(i, carry):
        r0 = pl.multiple_of(i * rb, rb)
        logf = _log_sigmoid(gates_ref[pl.ds(r0, rb), :] + bf_ref[...])
        c = _dot01_l(ltri_ref[...], logf) + carry
        hi, mid, lo = [t.astype(F32) for t in _split3(c)]
        c3 = jnp.where(lane < N_HEADS, hi, jnp.where(lane < 2 * N_HEADS, pltpu.roll(mid, N_HEADS, 1),
                                                       pltpu.roll(lo, 2 * N_HEADS, 1))).astype(BF16)
        qkv = qkv_ref[pl.ds(r0, rb), :]
        q = qkv[:, 0:GROUP_W]
        k = qkv[:, GROUP_W:2 * GROUP_W]
        v = qkv[:, 2 * GROUP_W:3 * GROUP_W]
        bd = bd_ref[...]
        qn = q * lax.rsqrt(_seg_meansq(q, bd, HEAD_DIM) + EPS) * qg_ref[...] * (HEAD_DIM ** -0.5)
        kn = k * lax.rsqrt(_seg_meansq(k, bd, HEAD_DIM) + EPS) * kg_ref[...]
        q_route = _dot(c3, rq_ref[...])
        k_route = _dot(c3, rk_ref[...])
        for h in range(N_HEADS):
            p = h // 2
            own = (lane // HEAD_DIM) == (h % 2)
            qa = jnp.where(own, qn[:, p * LANES:(p + 1) * LANES], q_route[:, h * LANES:(h + 1) * LANES] + oq_ref[h])
            ka = jnp.where(own, kn[:, p * LANES:(p + 1) * LANES], k_route[:, h * LANES:(h + 1) * LANES] + ok_ref[h])
            va = jnp.where(own, v[:, p * LANES:(p + 1) * LANES], jnp.where(lane == (1 - h % 2) * HEAD_DIM, 1.0, 0.0))
            qa_ref[0, h, pl.ds(r0, rb), :] = qa.astype(BF16)
            ka_ref[0, h, pl.ds(r0, rb), :] = ka.astype(BF16)
            va_ref[0, h, pl.ds(r0, rb), :] = va.astype(BF16)
        return c[rb - 1:rb, :]

    lax.fori_loop(0, s_len // rb, blk, jnp.zeros((1, GATE_W), F32))


def _fox_prep(fox_qkv, gates, bf_row, qg_row, kg_row, consts, b, s):
    rb = min(256, s)
    rq, rk, oq, ok = consts["fox_routes"]
    ltri = jnp.asarray(_tril_ones(rb), BF16)
    kern = functools.partial(_fox_prep_kernel, rb=rb)
    return pl.pallas_call(
        kern,
        grid=(b,),
        in_specs=[pl.BlockSpec((s, 3 * GROUP_W), lambda i: (i, 0)),
                  pl.BlockSpec((s, GATE_W), lambda i: (i, 0)),
                  _full_spec((1, GATE_W)), _full_spec((1, GROUP_W)), _full_spec((1, GROUP_W)),
                  _full_spec((GROUP_W, GROUP_W)), _full_spec((rb, rb)),
                  _full_spec(rq.shape), _full_spec(rk.shape), _full_spec(oq.shape), _full_spec(ok.shape)],
        out_specs=[pl.BlockSpec((1, N_HEADS, s, LANES), lambda i: (i, 0, 0, 0)),
                   pl.BlockSpec((1, N_HEADS, s, LANES), lambda i: (i, 0, 0, 0)),
                   pl.BlockSpec((1, N_HEADS, s, LANES), lambda i: (i, 0, 0, 0))],
        out_shape=[jax.ShapeDtypeStruct((b, N_HEADS, s, LANES), BF16)] * 3,
        compiler_params=_cparams(1),
        name="fox_prep",
    )(fox_qkv, gates, bf_row, qg_row, kg_row, consts["bd256"], ltri, rq, rk, oq, ok)


def _fox_attn_kernel(q_ref, k_ref, v_ref, og_ref, o_ref, s_scr, m_scr, acc_scr, *, blk, nq):
    lane = lax.broadcasted_iota(jnp.int32, (blk, LANES), 1)
    half = blk // 2
    row_t = lax.broadcasted_iota(jnp.int32, (half, half), 0)
    col_t = lax.broadcasted_iota(jnp.int32, (half, half), 1)
    row_b = lax.broadcasted_iota(jnp.int32, (half, blk), 0)
    col_b = lax.broadcasted_iota(jnp.int32, (half, blk), 1)

    def fold(s):
        m = s[:, 0:LANES]
        for c in range(1, s.shape[1] // LANES):
            m = jnp.maximum(m, s[:, c * LANES:(c + 1) * LANES])
        return m

    def logit_stages(i):
        slot = i % 2
        for j in range(i):
            for hh in range(2):
                s = _dot_nt(q_ref[0, hh, i * blk:(i + 1) * blk, :], k_ref[0, hh, j * blk:(j + 1) * blk, :])
                s_scr[slot, hh, j] = s
                m_scr[slot, hh] = fold(s) if j == 0 else jnp.maximum(m_scr[slot, hh], fold(s))
            yield
        for hh in range(2):
            q0 = i * blk
            top = _dot_nt(q_ref[0, hh, q0:q0 + half, :], k_ref[0, hh, q0:q0 + half, :])
            top = jnp.where(col_t <= row_t, top, NEG_BIG)
            bot = _dot_nt(q_ref[0, hh, q0 + half:q0 + blk, :], k_ref[0, hh, q0:q0 + blk, :])
            bot = jnp.where(col_b <= row_b + half, bot, NEG_BIG)
            s_scr[slot, hh, i, 0:half, 0:half] = top
            s_scr[slot, hh, i, half:blk, :] = bot
            m_top, m_bot = fold(top), fold(bot)
            if i > 0:
                m_top = jnp.maximum(m_scr[slot, hh, 0:half, :], m_top)
                m_bot = jnp.maximum(m_scr[slot, hh, half:blk, :], m_bot)
            m_scr[slot, hh, 0:half, :] = m_top
            m_scr[slot, hh, half:blk, :] = m_bot
        yield

    def value_stages(i):
        slot = i % 2
        row_max = [jnp.max(m_scr[slot, hh], axis=-1, keepdims=True) for hh in range(2)]
        for j in range(i):
            for hh in range(2):
                p = jnp.exp(s_scr[slot, hh, j] - row_max[hh])
                pv = _dot(p.astype(BF16), v_ref[0, hh, j * blk:(j + 1) * blk, :])
                acc_scr[slot, hh] = pv if j == 0 else acc_scr[slot, hh] + pv
            yield
        for hh in range(2):
            q0 = i * blk
            p_top = jnp.exp(s_scr[slot, hh, i, 0:half, 0:half] - row_max[hh][0:half])
            p_bot = jnp.exp(s_scr[slot, hh, i, half:blk, :] - row_max[hh][half:blk])
            pv_top = _dot(p_top.astype(BF16), v_ref[0, hh, q0:q0 + half, :])
            pv_bot = _dot(p_bot.astype(BF16), v_ref[0, hh, q0:q0 + blk, :])
            if i > 0:
                pv_top = acc_scr[slot, hh, 0:half, :] + pv_top
                pv_bot = acc_scr[slot, hh, half:blk, :] + pv_bot
            acc_scr[slot, hh, 0:half, :] = pv_top
            acc_scr[slot, hh, half:blk, :] = pv_bot
        yield
        outs = []
        for hh in range(2):
            acc = acc_scr[slot, hh]
            l = jnp.sum(jnp.where(lane == (1 - hh) * HEAD_DIM, acc, 0.0), axis=-1, keepdims=True)
            o = acc / l
            own = (lane // HEAD_DIM) == hh
            ms = jnp.sum(jnp.where(own, o * o, 0.0), axis=-1, keepdims=True) * (1.0 / HEAD_DIM)
            outs.append(o * lax.rsqrt(ms + EPS) * og_ref[...])
        o_ref[i * blk:(i + 1) * blk, :] = jnp.where((lane // HEAD_DIM) == 0, outs[0], outs[1]).astype(BF16)

    _interleave(logit_stages(0))
    for i in range(nq):
        if i + 1 < nq:
            _interleave(value_stages(i), logit_stages(i + 1))
        else:
            _interleave(value_stages(i))


def _fox_attn(qa, ka, va, og_row, b, s):
    blk = min(FOX_BLOCK, s)
    nq = s // blk
    kern = functools.partial(_fox_attn_kernel, blk=blk, nq=nq)
    head_pair = pl.BlockSpec((1, 2, s, LANES), lambda i, p: (i, p, 0, 0))
    return pl.pallas_call(
        kern,
        grid=(b, 2),
        in_specs=[head_pair, head_pair, head_pair, _full_spec((1, LANES))],
        out_specs=pl.BlockSpec((s, LANES), lambda i, p: (i, p)),
        out_shape=jax.ShapeDtypeStruct((b * s, GROUP_W), BF16),
        scratch_shapes=[pltpu.VMEM((2, 2, nq, blk, blk), F32),
                        pltpu.VMEM((2, 2, blk, LANES), F32),
                        pltpu.VMEM((2, 2, blk, LANES), F32)],
        compiler_params=_cparams(2),
        name="fox_attn",
    )(qa, ka, va, og_row)


def _gdn_kernel(qkv_ref, z_ref, gates_ref, cw_ref, alog_ref, dt_ref, og_ref, bd_ref, eb_ref, eg_ref,
                eye_ref, strict_ref, causal_ref, ones_ref, ctri_ref, lvl_ref,
                o_ref, xpad, q_scr, k_scr, v_scr, beta_scr, g_scr, o_scr, u_scr, s_scr,
                wq_scr, aqk_scr, kout_scr, *, rb):
    s_len = qkv_ref.shape[0]
    nblk = s_len // rb
    bd16 = bd_ref[...]
    bdf = bd16.astype(F32)

    xpad[0:8, :] = jnp.zeros((8, 3 * GROUP_W), F32)
    xpad[8:, :] = qkv_ref[...]

    def prep_stages(i):
        r0 = pl.multiple_of(i * rb, rb)

        gt = gates_ref[pl.ds(r0, rb), :]
        beta_scr[pl.ds(r0, rb), :] = _dot01_r(jax.nn.sigmoid(gt), eb_ref[...])
        g_s = -jnp.exp(alog_ref[...]) * _softplus(gt + dt_ref[...])
        g_rep = [_dot(t, eg_ref[...]).astype(BF16) for t in _split3(g_s)]

        def conv_silu(c0):
            xx = xpad[pl.ds(r0, rb + 8), c0:c0 + LANES]
            y = cw_ref[GDN_CONV - 1:GDN_CONV, c0:c0 + LANES] * xx[8:8 + rb]
            for j in range(GDN_CONV - 1):
                y = y + cw_ref[j:j + 1, c0:c0 + LANES] * pltpu.roll(xx, GDN_CONV - 1 - j, 0)[8:8 + rb]
            return _silu(y)

        yield
        q_lo = conv_silu(0)
        yield
        q = jnp.concatenate([q_lo, conv_silu(LANES)], axis=1)
        q_ss = _seg_meansq(q, bd16, 1.0)
        yield
        k_lo = conv_silu(2 * LANES)
        yield
        k = jnp.concatenate([k_lo, conv_silu(3 * LANES)], axis=1)
        k_ss = _seg_meansq(k, bd16, 1.0)
        yield
        v_scr[pl.ds(r0, rb), 0:LANES] = conv_silu(4 * LANES)
        yield
        v_scr[pl.ds(r0, rb), LANES:2 * LANES] = conv_silu(5 * LANES)
        ct = ctri_ref[...]
        g_scr[pl.ds(r0, rb), :] = _dot(ct, g_rep[0]) + _dot(ct, g_rep[1]) + _dot(ct, g_rep[2])
        yield
        q_scr[pl.ds(r0, rb), :] = q * lax.rsqrt(q_ss + EPS) * (HEAD_DIM ** -0.5)
        k_scr[pl.ds(r0, rb), :] = k * lax.rsqrt(k_ss + EPS)

    def bd(y16):
        return _tile_rows(y16, N_HEADS) * bd16

    eye = eye_ref[...]

    def wy_stages(i):
        ns = [i * WY_GROUP + c for c in range(WY_GROUP)]
        r0s = [pl.multiple_of(n * CHUNK, CHUNK) for n in ns]
        q = [q_scr[pl.ds(r0, CHUNK), :] for r0 in r0s]
        k = [k_scr[pl.ds(r0, CHUNK), :] for r0 in r0s]
        beta = [beta_scr[pl.ds(r0, CHUNK), :] for r0 in r0s]
        g = [g_scr[pl.ds(r0, CHUNK), :] for r0 in r0s]
        kb = [kc * bc for kc, bc in zip(k, beta)]
        aa = [_dot_nt(jnp.concatenate([kbc.astype(BF16), qc.astype(BF16)], axis=0), bd(kc.astype(BF16)))
              for kbc, qc, kc in zip(kb, q, k)]
        g_row = [_dot01_l(ones_ref[...], gc * eye) for gc in g]
        yield
        decay = [jnp.exp(jnp.minimum(gc - grc, 0.0)) for gc, grc in zip(g, g_row)]
        m = [ac[0:CHUNK] * dc * strict_ref[...] for ac, dc in zip(aa, decay)]
        for r0, ac, dc in zip(r0s, aa, decay):
            aqk_scr[pl.ds(r0, CHUNK), :] = (ac[CHUNK:2 * CHUNK] * dc * causal_ref[...]).astype(BF16)
        pm = [eye - mc * lvl_ref[0] for mc in m]
        for li in range(1, len(LEVELS)):
            x16 = [pc.astype(BF16) for pc in pm]
            xc = [_dot(xc16, bd((mc * lvl_ref[li]).astype(BF16))) for xc16, mc in zip(x16, m)]
            yield
            pm = [pc - _dot(t.astype(BF16), bd(xc16)) for pc, t, xc16 in zip(pm, xc, x16)]
            yield
        t16 = [pc.astype(BF16) for pc in pm]
        eg = [jnp.exp(gc) for gc in g]
        for c, (n, r0) in enumerate(zip(ns, r0s)):
            v = v_scr[pl.ds(r0, CHUNK), :]
            g_last = g_scr[pl.ds(r0 + CHUNK - 1, 1), :]
            wq_scr[n, 0:CHUNK, :] = _dot(t16[c], bd((kb[c] * eg[c]).astype(BF16))).astype(BF16)
            wq_scr[n, CHUNK:2 * CHUNK, :] = (q[c] * eg[c]).astype(BF16)
            u_scr[pl.ds(r0, CHUNK), :] = _dot(t16[c], bd((v * beta[c]).astype(BF16)))
            kout_scr[pl.ds(r0, CHUNK), :] = (k[c] * jnp.exp(g_last - g[c])).astype(BF16)

    def scan_stages(i):
        st = s_scr[...]
        for c in range(WY_GROUP):
            n = i * WY_GROUP + c
            r0 = pl.multiple_of(n * CHUNK, CHUNK)
            wq = _dot(wq_scr[n], st.astype(BF16))
            yield
            v16 = (u_scr[pl.ds(r0, CHUNK), :] - wq[0:CHUNK]).astype(BF16)
            a_last = jnp.exp(g_scr[pl.ds(r0 + CHUNK - 1, 1), :])
            st = st * a_last + _dot_tn(kout_scr[pl.ds(r0, CHUNK), :], v16) * bdf
            o_scr[pl.ds(r0, CHUNK), :] = wq[CHUNK:2 * CHUNK] + _dot(aqk_scr[pl.ds(r0, CHUNK), :], bd(v16))
            yield
        s_scr[...] = st

    def out_stages(i):
        r0 = pl.multiple_of(i * rb, rb)
        o = o_scr[pl.ds(r0, rb), :]
        o_ms = _seg_meansq(o, bd16, HEAD_DIM)
        yield
        on = o * lax.rsqrt(o_ms + EPS) * og_ref[...]
        o_ref[pl.ds(r0, rb), :] = (on * _silu(z_ref[pl.ds(r0, rb), :])).astype(BF16)

    s_scr[...] = jnp.zeros((GROUP_W, GROUP_W), F32)
    _software_pipeline((prep_stages, wy_stages, scan_stages, out_stages), nblk)


def _gdn(gdn_qkv, gdn_z, gates, cw, alog_row, dt_row, og_row, consts, b, s):
    rb = min(256, s)
    eye, strict, causal = consts["chunk_masks"]
    kern = functools.partial(_gdn_kernel, rb=rb)
    row = lambda w: pl.BlockSpec((s, w), lambda i: (i, 0))
    return pl.pallas_call(
        kern,
        grid=(b,),
        in_specs=[row(3 * GROUP_W), row(GROUP_W), row(GATE_W),
                  _full_spec((8, 3 * GROUP_W)), _full_spec((1, GATE_W)), _full_spec((1, GATE_W)),
                  _full_spec((1, GROUP_W)), _full_spec((GROUP_W, GROUP_W)),
                  _full_spec((GATE_W, GROUP_W)), _full_spec((GATE_W, GROUP_W)),
                  _full_spec((CHUNK, GROUP_W)), _full_spec((CHUNK, GROUP_W)), _full_spec((CHUNK, GROUP_W)),
                  _full_spec((CHUNK, CHUNK)), _full_spec((rb, rb)), _full_spec((len(LEVELS), CHUNK, GROUP_W))],
        out_specs=row(GROUP_W),
        out_shape=jax.ShapeDtypeStruct((b * s, GROUP_W), BF16),
        scratch_shapes=[pltpu.VMEM((s + 8, 3 * GROUP_W), F32)]
                       + [pltpu.VMEM((s, GROUP_W), F32) for _ in range(7)]
                       + [pltpu.VMEM((GROUP_W, GROUP_W), F32),
                          pltpu.VMEM((s // CHUNK, 2 * CHUNK, GROUP_W), BF16),
                          pltpu.VMEM((s, GROUP_W), BF16), pltpu.VMEM((s, GROUP_W), BF16)],
        compiler_params=_cparams(1),
        name="gdn",
    )(gdn_qkv, gdn_z, gates, cw, alog_row, dt_row, og_row, consts["bd256"], consts["expand_b"], consts["expand_g"],
      eye, strict, causal, consts["ones64"],
      jnp.asarray(_tril_ones(rb) * _block_mask(rb, CHUNK, rb, CHUNK), BF16), consts["level_masks"])


def _la_kernel(*refs, variant, rb):
    if variant == "hgrn2":
        (x_ref, la_ref, l1_ref, oml_ref, og_ref, bdv_ref, bdk_ref, eye_ref, lvl_ref, ltri_ref,
         o_ref, q_scr, k_scr, a_scr, v_scr, gate_scr, gc_scr, o_scr, st_scr) = refs
        kw, dk = GROUP_W, HEAD_DIM
    else:
        (x_ref, gates_ref, wgk_ref, bgk_ref, og_ref, bdv_ref, bdk_ref, eye_ref, lvl_ref, ltri_ref,
         o_ref, q_scr, k_scr, a_scr, v_scr, gate_scr, gc_scr, o_scr, st_scr) = refs
        kw, dk = GLA_KW, GLA_DK
    s_len = x_ref.shape[0]
    nblk = s_len // rb
    bdv16 = bdv_ref[...]
    bdk16 = bdk_ref[...]
    bdkf = bdk16.astype(F32)

    def prep_stages(i):
        r0 = pl.multiple_of(i * rb, rb)
        if variant == "hgrn2":
            f_logit = x_ref[pl.ds(r0, rb), GROUP_W:2 * GROUP_W]
            q_scr[pl.ds(r0, rb), :] = _silu(x_ref[pl.ds(r0, rb), 0:GROUP_W]) * (dk ** -0.5)
            yield
            a = la_ref[...]
            bterm = l1_ref[...] + _log_sigmoid(f_logit)
            amax = jnp.maximum(a, bterm)
            a_scr[pl.ds(r0, rb), :] = amax + jnp.log1p(jnp.exp(-jnp.abs(a - bterm)))
            yield
            k_scr[pl.ds(r0, rb), :] = oml_ref[...] * jax.nn.sigmoid(-f_logit)
            yield
            v_scr[pl.ds(r0, rb), :] = x_ref[pl.ds(r0, rb), 2 * GROUP_W:3 * GROUP_W]
            gate_scr[pl.ds(r0, rb), :] = x_ref[pl.ds(r0, rb), 3 * GROUP_W:4 * GROUP_W]
        else:
            lr = _dot(gates_ref[pl.ds(r0, rb), :].astype(BF16), wgk_ref[...]) + bgk_ref[...]
            q_scr[pl.ds(r0, rb), :] = x_ref[pl.ds(r0, rb), 0:kw] * (dk ** -0.5)
            k_scr[pl.ds(r0, rb), :] = x_ref[pl.ds(r0, rb), kw:2 * kw]
            yield
            v_scr[pl.ds(r0, rb), :] = x_ref[pl.ds(r0, rb), 2 * kw:2 * kw + GROUP_W]
            gate_scr[pl.ds(r0, rb), :] = x_ref[pl.ds(r0, rb), 2 * kw + GROUP_W:2 * kw + 2 * GROUP_W]
            yield
            a_scr[pl.ds(r0, rb), :] = _log_sigmoid(lr) * (1.0 / GLA_NORM)

    sub = lax.broadcasted_iota(jnp.int32, (8, kw), 0)

    def level_ref(m, slot):
        def brow(r):
            return jnp.broadcast_to(gc_scr[slot, r:r + 1, :], (8, kw))
        pieces = []
        for a in range(CHUNK // 8):
            if 2 * m >= 8:
                pieces.append(brow((8 * a) // (2 * m) * (2 * m) + m - 1))
            elif m == 2:
                pieces.append(jnp.where(sub < 4, brow(8 * a + 1), brow(8 * a + 5)))
            else:
                p = jnp.where(sub < 2, brow(8 * a), brow(8 * a + 2))
                p = jnp.where(sub < 4, p, jnp.where(sub < 6, brow(8 * a + 4), brow(8 * a + 6)))
                pieces.append(p)
        return jnp.concatenate(pieces, axis=0)

    def chunk_stages(i):
        slots = range(rb // CHUNK)
        r0s = [pl.multiple_of(i * rb + c * CHUNK, CHUNK) for c in slots]
        q = [q_scr[pl.ds(r0, CHUNK), :] for r0 in r0s]
        k = [k_scr[pl.ds(r0, CHUNK), :] for r0 in r0s]
        v16 = [v_scr[pl.ds(r0, CHUNK), :].astype(BF16) for r0 in r0s]
        g = [_dot01_l(ltri_ref[...], a_scr[pl.ds(r0, CHUNK), :]) * LOG2_E for r0 in r0s]
        yield
        for c in slots:
            gc_scr[c] = g[c]
        q16 = [qc.astype(BF16) for qc in q]
        k16 = [kc.astype(BF16) for kc in k]
        a_in = [_dot_nt(q16[c], _tile_rows(k16[c], N_HEADS) * bdk16) * eye_ref[...] for c in slots]
        yield
        for li, m in enumerate(LEVELS):
            for c in slots:
                e = jnp.exp2(-jnp.abs(g[c] - level_ref(m, c))).astype(BF16)
                a_in[c] = a_in[c] + _dot_nt(q16[c] * e, _tile_rows(k16[c] * e, N_HEADS) * bdk16) * lvl_ref[li]
            yield
        o = [_dot(a_in[c].astype(BF16), _tile_rows(v16[c], N_HEADS) * bdv16) for c in slots]
        ds = [_dot_tn(v16[c], (k[c] * jnp.exp2(g[c][CHUNK - 1:CHUNK, :] - g[c])).astype(BF16)) * bdkf for c in slots]
        q_in = [(q[c] * jnp.exp2(g[c])).astype(BF16) for c in slots]
        yield
        st = st_scr[...]
        for c in slots:
            o_scr[pl.ds(r0s[c], CHUNK), :] = o[c] + _dot_nt(q_in[c], st.astype(BF16))
            st = st * jnp.exp2(g[c][CHUNK - 1:CHUNK, :]) + ds[c]
        st_scr[...] = st

    def out_stages(i):
        r0 = pl.multiple_of(i * rb, rb)
        o = o_scr[pl.ds(r0, rb), :]
        o_ms = _seg_meansq(o, bdv16, HEAD_DIM)
        yield
        on = o * lax.rsqrt(o_ms + EPS) * og_ref[...]
        o_ref[pl.ds(r0, rb), :] = (on * _silu(gate_scr[pl.ds(r0, rb), :])).astype(BF16)

    st_scr[...] = jnp.zeros((GROUP_W, kw), F32)
    _software_pipeline((prep_stages, chunk_stages, out_stages), nblk)


def _la(variant, x, extra, og_row, consts, b, s):
    rb = min(LA_GROUP_ROWS[variant], s)
    kw = GROUP_W if variant == "hgrn2" else GLA_KW
    xw = 4 * GROUP_W if variant == "hgrn2" else 3 * GROUP_W
    kern = functools.partial(_la_kernel, variant=variant, rb=rb)
    row = lambda w: pl.BlockSpec((s, w), lambda i: (i, 0))
    if variant == "hgrn2":
        in_specs = [row(xw)] + [_full_spec((1, GROUP_W))] * 3
        bdk = consts["bd256"]
    else:
        in_specs = [row(xw), row(GATE_W), _full_spec((GATE_W, GLA_KW)), _full_spec((1, GLA_KW))]
        bdk = consts["bdk_gla"]
    in_specs += [_full_spec((1, GROUP_W)), _full_spec((GROUP_W, GROUP_W)), _full_spec((GROUP_W, kw)),
                 _full_spec((CHUNK, GROUP_W)), _full_spec((len(LEVELS), CHUNK, GROUP_W)), _full_spec((CHUNK, CHUNK))]
    return pl.pallas_call(
        kern,
        grid=(b,),
        in_specs=in_specs,
        out_specs=row(GROUP_W),
        out_shape=jax.ShapeDtypeStruct((b * s, GROUP_W), BF16),
        scratch_shapes=[pltpu.VMEM((s, kw), F32) for _ in range(3)]
                       + [pltpu.VMEM((s, GROUP_W), F32) for _ in range(2)]
                       + [pltpu.VMEM((rb // CHUNK, CHUNK, kw), F32), pltpu.VMEM((s, GROUP_W), F32),
                          pltpu.VMEM((GROUP_W, kw), F32)],
        compiler_params=_cparams(1),
        name="la_" + variant,
    )(x, *extra, og_row, consts["bd256"], bdk, consts["chunk_masks"][0], consts["level_masks"], consts["ltri64"])


FFN_TAIL = 16
FFN_LATE_CHUNKS = 2


def _ffn_kernel(x_ref, oa_ref, ob_ref, oc_ref, od_ref, wo_ref, g2_ref, wup_ref, ctap_ref, wd_ref,
                out_ref, hext, act_scr, *, tm):
    it = pl.program_id(1)
    mixed = jnp.concatenate([oa_ref[...], ob_ref[...], oc_ref[...], od_ref[...]], axis=1)
    x1 = x_ref[...] + _dot(mixed, wo_ref[...])
    out_ref[...] = x1
    ms = jnp.mean(x1 * x1, axis=-1, keepdims=True)
    h2 = (x1 * lax.rsqrt(ms + EPS) * g2_ref[...]).astype(BF16)

    @pl.when(it == 0)
    def _():
        hext[0:FFN_TAIL, :] = jnp.zeros((FFN_TAIL, D_MODEL), BF16)

    hext[FFN_TAIL:, :] = h2

    def conv(col0):
        u = _dot(hext[...], wup_ref[:, col0:col0 + FF_CHUNK])
        c = ctap_ref[:, col0:col0 + FF_CHUNK]
        r1 = pltpu.roll(u, 1, 0)
        r2 = pltpu.roll(u, 2, 0)
        y = c[0:1, :] * r2[FFN_TAIL:FFN_TAIL + tm] + c[1:2, :] * r1[FFN_TAIL:FFN_TAIL + tm]
        return y + c[2:3, :] * u[FFN_TAIL:FFN_TAIL + tm] + c[3:4, :]

    for j in range(D_FF // FF_CHUNK):
        gate = conv(j * FF_CHUNK)
        up = conv(D_FF + j * FF_CHUNK)
        act_scr[:, j * FF_CHUNK:(j + 1) * FF_CHUNK] = (_silu(gate) * up).astype(BF16)
    k1 = (D_FF // FF_CHUNK - FFN_LATE_CHUNKS) * FF_CHUNK
    out_ref[...] += _dot(act_scr[:, 0:k1], wd_ref[0:k1, :])
    out_ref[...] += _dot(act_scr[:, k1:D_FF], wd_ref[k1:D_FF, :])
    hext[0:FFN_TAIL, :] = hext[tm:tm + FFN_TAIL, :]


def _ffn(x2d, outs, wo, g2, wup, ctap, wd, b, s):
    tm = min(FFN_ROW_TILE, s)
    nt = s // tm
    kern = functools.partial(_ffn_kernel, tm=tm)
    row = lambda w: pl.BlockSpec((tm, w), lambda i, j: (i * nt + j, 0))
    return pl.pallas_call(
        kern,
        grid=(b, nt),
        in_specs=[row(D_MODEL)] + [row(GROUP_W)] * 4
                 + [_full_spec((D_MODEL, D_MODEL)), _full_spec((1, D_MODEL)),
                    _full_spec((D_MODEL, 2 * D_FF)), _full_spec((8, 2 * D_FF)), _full_spec((D_FF, D_MODEL))],
        out_specs=row(D_MODEL),
        out_shape=jax.ShapeDtypeStruct(x2d.shape, F32),
        scratch_shapes=[pltpu.VMEM((tm + FFN_TAIL, D_MODEL), BF16),
                        pltpu.VMEM((tm, D_FF), BF16)],
        compiler_params=_cparams(2),
        name="outproj_ffn",
    )(x2d, *outs, wo, g2, wup, ctap, wd)


def _constants():
    eye, strict, causal = _chunk_masks()
    return {
        "bd256": jnp.asarray(_block_mask(GROUP_W, HEAD_DIM, GROUP_W, HEAD_DIM), BF16),
        "bdk_gla": jnp.asarray(_block_mask(GROUP_W, HEAD_DIM, GLA_KW, GLA_DK), BF16),
        "expand_b": jnp.asarray(_expand_mat(GATE_GDN_B), BF16),
        "expand_g": jnp.asarray(_expand_mat(GATE_GDN_A), BF16),
        "chunk_masks": (jnp.asarray(eye), jnp.asarray(strict), jnp.asarray(causal)),
        "level_masks": jnp.asarray(_level_masks()),
        "ones64": jnp.ones((CHUNK, CHUNK), BF16),
        "ltri64": jnp.asarray(_tril_ones(CHUNK), BF16),
        "fox_routes": tuple(jnp.asarray(a, BF16) if a.ndim == 2 else jnp.asarray(a) for a in _fox_routes()),
    }


def _w_in_moves():
    sizes = (3 * GROUP_W, N_HEADS, 3 * GROUP_W, N_HEADS, N_HEADS, GROUP_W,
             GROUP_W, GROUP_W, GROUP_W, GROUP_W, 2 * GLA_KW, GROUP_W, GLA_RANK, GROUP_W)
    names = ("fox_qkv", "fox_f", "gdn_qkv", "gdn_b", "gdn_a", "gdn_z", "hg_q", "hg_f", "hg_i", "hg_g",
             "gla_qk", "gla_v", "gla_gk", "gla_g")
    src = dict(zip(names, np.concatenate([[0], np.cumsum(sizes)[:-1]])))
    wid = dict(zip(names, sizes))
    order = ("fox_qkv", "gdn_qkv", "gdn_z", "hg_q", "hg_f", "hg_i", "hg_g", "gla_qk", "gla_v", "gla_g",
             "fox_f", "gdn_b", "gdn_a", "gla_gk")
    moves, dst = [], 0
    for nm in order:
        moves.append((int(src[nm]), dst, wid[nm]))
        dst += wid[nm]
    return moves, dst


def _regroup_kernel(w_ref, o_ref):
    moves, used = _w_in_moves()
    x = w_ref[0]
    for src, dst, wd in moves:
        o_ref[0, :, dst:dst + wd] = x[:, src:src + wd].astype(BF16)
    o_ref[0, :, used:IN_COLS_PAD] = jnp.zeros((x.shape[0], IN_COLS_PAD - used), BF16)


def _regroup_w_in(w_in):
    depth, d, cols = w_in.shape
    tr = 256
    return pl.pallas_call(
        _regroup_kernel,
        grid=(depth, d // tr),
        in_specs=[pl.BlockSpec((1, tr, cols), lambda l, i: (l, i, 0))],
        out_specs=pl.BlockSpec((1, tr, IN_COLS_PAD), lambda l, i: (l, i, 0)),
        out_shape=jax.ShapeDtypeStruct((depth, d, IN_COLS_PAD), BF16),
        compiler_params=_cparams(2),
        name="regroup_w_in",
    )(w_in)


def _lane_row(vals, lane0, width):
    return jnp.zeros((1, width), F32).at[0, lane0:lane0 + vals.shape[0]].set(vals.astype(F32))


def _tile_heads(g):
    return jnp.tile(g.astype(F32), N_HEADS)[None, :]


def kernel(x, norm1_g, w_in, fox_qn_g, fox_kn_g, fox_b_f, fox_on_g, gdn_conv_w, gdn_a_log, gdn_dt_bias, gdn_on_g,
           hg_lb, hg_on_g, gla_w_gk, gla_b_gk, gla_on_g, w_out, norm2_g, w_up, ffn_conv_w, ffn_conv_b, w_down):
    b, s, d = x.shape
    depth = w_in.shape[0]
    consts = _constants()

    cs = jnp.cumsum(jax.nn.softmax(hg_lb.astype(F32), axis=0), axis=0)
    lower = cs - cs[0:1]

    w_in_all = _regroup_w_in(w_in)
    x2d = x.reshape(b * s, d)
    for l in range(depth):
        fox_qkv, gdn_qkv, gdn_z, hg, gla, gates = _inproj(x2d, norm1_g[l][None, :], w_in_all, l)

        qa, ka, va = _fox_prep(fox_qkv, gates, _lane_row(fox_b_f[l], GATE_FOX_F, GATE_W),
                               _tile_heads(fox_qn_g[l]), _tile_heads(fox_kn_g[l]), consts, b, s)
        o_a = _fox_attn(qa, ka, va, jnp.tile(fox_on_g[l].astype(F32), 2)[None, :], b, s)

        cw = jnp.zeros((8, 3 * GROUP_W), F32).at[0:GDN_CONV].set(gdn_conv_w[l])
        o_b = _gdn(gdn_qkv, gdn_z, gates, cw, _lane_row(gdn_a_log[l], GATE_GDN_A, GATE_W),
                   _lane_row(gdn_dt_bias[l], GATE_GDN_A, GATE_W), _tile_heads(gdn_on_g[l]), consts, b, s)

        lb = lower[l][None, :]
        o_c = _la("hgrn2", hg, (jnp.log(lb), jnp.log1p(-lb), 1.0 - lb), _tile_heads(hg_on_g[l]), consts, b, s)

        wgk = jnp.zeros((GATE_W, GLA_KW), F32).at[GATE_GLA_GK:GATE_GLA_GK + GLA_RANK].set(gla_w_gk[l]).astype(BF16)
        o_d = _la("gla", gla, (gates, wgk, gla_b_gk[l][None, :].astype(F32)), _tile_heads(gla_on_g[l]), consts, b, s)

        wo = w_out[l].astype(BF16)
        taps = jnp.concatenate([ffn_conv_w[l], ffn_conv_b[l][None, :],
                                jnp.zeros((8 - FFN_CONV - 1, 2 * D_FF), F32)], axis=0)
        x2d = _ffn(x2d, (o_a, o_b, o_c, o_d), wo, norm2_g[l][None, :], w_up[l].astype(BF16), taps,
                   w_down[l].astype(BF16), b, s)
    return x2d.reshape(b, s, d)
```
